```python
import math
import jax, jax.numpy as jnp
from jax import lax
import numpy as np

D_MODEL = 2048
BATCH = 1
SEQ = 8192
DEPTH = 4

N_MIXERS = 3
N_MOBA = (DEPTH + 2) // 3
N_MLA = (DEPTH + 1) // 3
N_DIL = DEPTH // 3

ROPE_THETA = 10000.0
NORM_EPS = 1e-6

N_HEADS = 16
HEAD_DIM = D_MODEL // N_HEADS
MOBA_BLOCK = 256
MOBA_TOPK = 3
MOBA_QCHUNK = 32

MLA_HEADS = 16
MLA_Q_RANK = 512
MLA_KV_RANK = 512
MLA_NOPE_DIM = 128
MLA_ROPE_DIM = 64
MLA_V_DIM = 128
MLA_QBLOCK = 128

DIL_PATTERNS = ((128, 1), (512, 4), (2048, 16))

D_FF = 5632
CONV_WIDTH = 3

kernel_name = "hybrid_moba_mla_dilated_convffn"

F32 = jnp.float32


def rms_norm(x, g):
    xf = x.astype(F32)
    y = xf * lax.rsqrt(jnp.mean(xf * xf, axis=-1, keepdims=True) + NORM_EPS)
    return (y * g.astype(F32)).astype(x.dtype)


def rope_tables(n_pos, dim):
    inv = 1.0 / (ROPE_THETA ** (jnp.arange(0, dim, 2, dtype=F32) / dim))
    ang = jnp.arange(n_pos, dtype=F32)[:, None] * inv[None, :]
    return jnp.cos(ang), jnp.sin(ang)


def apply_rope(x, cos, sin):
    x1, x2 = jnp.split(x, 2, axis=-1)
    c = cos[None, :, None, :].astype(x.dtype)
    s = sin[None, :, None, :].astype(x.dtype)
    return jnp.concatenate([x1 * c - x2 * s, x1 * s + x2 * c], axis=-1)


def qkv_heads(x, w_qkv, cos, sin):
    B, S, _ = x.shape
    q, k, v = jnp.split(x @ w_qkv, 3, axis=-1)
    q = apply_rope(q.reshape(B, S, N_HEADS, HEAD_DIM), cos, sin)
    k = apply_rope(k.reshape(B, S, N_HEADS, HEAD_DIM), cos, sin)
    return q, k, v.reshape(B, S, N_HEADS, HEAD_DIM)


def moba_attention(x, w_qkv, w_o, cos, sin):
    B, S, _ = x.shape
    H, Dh, BS, QC = N_HEADS, HEAD_DIM, MOBA_BLOCK, MOBA_QCHUNK
    q, k, v = qkv_heads(x, w_qkv, cos, sin)
    Sp = -(-S // BS) * BS
    pad = ((0, 0), (0, Sp - S), (0, 0), (0, 0))
    q, k, v = jnp.pad(q, pad), jnp.pad(k, pad), jnp.pad(v, pad)
    NB = Sp // BS
    kb = k.reshape(B, NB, BS, H, Dh).transpose(0, 3, 1, 2, 4)
    vb = v.reshape(B, NB, BS, H, Dh).transpose(0, 3, 1, 2, 4)
    k_mean = jnp.mean(kb.astype(F32), axis=3).astype(k.dtype)
    gate = jnp.einsum('bshd,bhnd->bshn', q, k_mean, preferred_element_type=F32)
    q_blk = jnp.arange(Sp) // BS
    past = jnp.arange(NB)[None, :] < q_blk[:, None]
    gate = jnp.where(past[None, :, None, :], gate, -jnp.inf)
    n_sel = min(MOBA_TOPK, NB)
    _, sel = lax.top_k(gate, n_sel)
    sel_valid = jnp.arange(n_sel)[None, :] < q_blk[:, None]

    n_chunks = Sp // QC
    qc = q.reshape(B, n_chunks, QC, H, Dh).transpose(1, 0, 2, 3, 4)
    selc = sel.reshape(B, n_chunks, QC, H, n_sel).transpose(1, 0, 2, 3, 4)
    validc = sel_valid.reshape(n_chunks, QC, n_sel)
    b_idx = jnp.arange(B)[:, None, None, None]
    h_idx = jnp.arange(H)[None, None, :, None]
    scale = Dh ** -0.5

    def one_chunk(args):
        c, qq, ss, vv = args
        start = c * QC
        blk = start // BS
        k_own = lax.dynamic_index_in_dim(kb, blk, axis=2, keepdims=False)
        v_own = lax.dynamic_index_in_dim(vb, blk, axis=2, keepdims=False)
        k_g = kb[b_idx, h_idx, ss]
        v_g = vb[b_idx, h_idx, ss]
        s_own = jnp.einsum('bqhd,bhkd->bqhk', qq, k_own, preferred_element_type=F32) * scale
        qpos = start + jnp.arange(QC)
        kpos = blk * BS + jnp.arange(BS)
        causal = kpos[None, :] <= qpos[:, None]
        s_own = jnp.where(causal[None, :, None, :], s_own, -jnp.inf)
        s_g = jnp.einsum('bqhd,bqhnkd->bqhnk', qq, k_g, preferred_element_type=F32) * scale
        s_g = jnp.where(vv[None, :, None, :, None], s_g, -jnp.inf)
        s = jnp.concatenate([s_own, s_g.reshape(B, QC, H, n_sel * BS)], axis=-1)
        p = jax.nn.softmax(s, axis=-1).astype(v.dtype)
        p_own = p[..., :BS]
        p_g = p[..., BS:].reshape(B, QC, H, n_sel, BS)
        return (jnp.einsum('bqhk,bhkd->bqhd', p_own, v_own)
                + jnp.einsum('bqhnk,bqhnkd->bqhd', p_g, v_g))

    out = lax.map(one_chunk, (jnp.arange(n_chunks), qc, selc, validc))
    out = out.transpose(1, 0, 2, 3, 4).reshape(B, Sp, H * Dh)[:, :S]
    return out @ w_o


def mla_attention(x, w_down, q_norm, kv_norm, w_uq, w_ukv, w_o, cos_r, sin_r):
    B, S, _ = x.shape
    H, QB = MLA_HEADS, MLA_QBLOCK
    lat = x @ w_down
    cq, ckv, k_rope = jnp.split(lat, [MLA_Q_RANK, MLA_Q_RANK + MLA_KV_RANK], axis=-1)
    cq = rms_norm(cq, q_norm)
    ckv = rms_norm(ckv, kv_norm)
    q = (cq @ w_uq).reshape(B, S, H, MLA_NOPE_DIM + MLA_ROPE_DIM)
    q_nope, q_rope = jnp.split(q, [MLA_NOPE_DIM], axis=-1)
    q_rope = apply_rope(q_rope, cos_r, sin_r)
    k_rope = apply_rope(k_rope[:, :, None, :], cos_r, sin_r)[:, :, 0]
    kv = (ckv @ w_ukv).reshape(B, S, H, MLA_NOPE_DIM + MLA_V_DIM)
    k_nope, v = jnp.split(kv, [MLA_NOPE_DIM], axis=-1)
    scale = (MLA_NOPE_DIM + MLA_ROPE_DIM) ** -0.5
    nq = S // QB
    qn = q_nope.reshape(B, nq, QB, H, MLA_NOPE_DIM).transpose(1, 0, 2, 3, 4)
    qr = q_rope.reshape(B, nq, QB, H, MLA_ROPE_DIM).transpose(1, 0, 2, 3, 4)
    kpos = jnp.arange(S)

    def one_block(args):
        i, qn_i, qr_i = args
        s = (jnp.einsum('bqhd,bkhd->bhqk', qn_i, k_nope, preferred_element_type=F32)
             + jnp.einsum('bqhr,bkr->bhqk', qr_i, k_rope, preferred_element_type=F32)) * scale
        qpos = i * QB + jnp.arange(QB)
        s = jnp.where((kpos[None, :] <= qpos[:, None])[None, None], s, -jnp.inf)
        p = jax.nn.softmax(s, axis=-1).astype(v.dtype)
        return jnp.einsum('bhqk,bkhd->bqhd', p, v)

    out = lax.map(one_block, (jnp.arange(nq), qn, qr))
    out = out.transpose(1, 0, 2, 3, 4).reshape(B, S, H * MLA_V_DIM)
    return out @ w_o


def strided_window_attention(q, k, v, window, dil):
    B, S, H, Dh = q.shape
    L = window // dil
    unit = dil * L
    Sp = -(-S // unit) * unit
    pad = ((0, 0), (0, Sp - S), (0, 0), (0, 0))
    n = Sp // dil
    nb = n // L

    def to_residue_blocks(t):
        t = jnp.pad(t, pad).reshape(B, n, dil, H, Dh).transpose(0, 2, 1, 3, 4)
        return t.reshape(B, dil, nb, L, H, Dh)

    qb, kb, vb = to_residue_blocks(q), to_residue_blocks(k), to_residue_blocks(v)
    prev = ((0, 0), (0, 0), (1, 0), (0, 0), (0, 0), (0, 0))
    k_band = jnp.concatenate([jnp.pad(kb, prev)[:, :, :-1], kb], axis=3)
    v_band = jnp.concatenate([jnp.pad(vb, prev)[:, :, :-1], vb], axis=3)
    s = jnp.einsum('brnqhd,brnkhd->brnhqk', qb, k_band, preferred_element_type=F32) * (Dh ** -0.5)
    qi = jnp.arange(L)[:, None]
    kk = jnp.arange(2 * L)[None, :]
    dist = qi + L - kk
    band = (dist >= 0) & (dist <= L)
    first = (jnp.arange(nb) == 0)[:, None, None]
    valid = band[None] & ~(first & (kk < L)[None])
    s = jnp.where(valid[None, None, :, None], s, -jnp.inf)
    lse = jax.nn.logsumexp(s, axis=-1)
    p = jnp.exp(s - lse[..., None]).astype(v.dtype)
    o = jnp.einsum('brnhqk,brnkhd->brnqhd', p, v_band)
    o = o.reshape(B, dil, n, H, Dh).transpose(0, 2, 1, 3, 4).reshape(B, Sp, H, Dh)[:, :S]
    lse = lse.transpose(0, 1, 2, 4, 3).reshape(B, dil, n, H).transpose(0, 2, 1, 3).reshape(B, Sp, H)[:, :S]
    return o, lse


def dilated_attention(x, w_qkv, w_o, cos, sin):
    B, S, _ = x.shape
    q, k, v = qkv_heads(x, w_qkv, cos, sin)
    outs, lses = [], []
    for window, dil in DIL_PATTERNS:
        o, lse = strided_window_attention(q, k, v, window, dil)
        outs.append(o)
        lses.append(lse)
    w = jax.nn.softmax(jnp.stack(lses, axis=0), axis=0).astype(v.dtype)
    out = jnp.einsum('pbsh,pbshd->bshd', w, jnp.stack(outs, axis=0))
    return out.reshape(B, S, N_HEADS * HEAD_DIM) @ w_o


def conv_glu_ffn(x, w_up, conv_w, conv_b, w_down):
    h = x @ w_up
    C = h.shape[-1]
    h = lax.conv_general_dilated(
        h, conv_w[:, None, :], window_strides=(1,), padding=((CONV_WIDTH - 1, 0),),
        dimension_numbers=('NWC', 'WIO', 'NWC'), feature_group_count=C) + conv_b
    gate, val = jnp.split(h, 2, axis=-1)
    return (jax.nn.silu(gate) * val) @ w_down


def setup_inputs(seed: int = 0) -> dict:
    key = jax.random.key(seed)
    ks = jax.random.split(key, 20)
    D = D_MODEL
    AW = N_HEADS * HEAD_DIM

    def w(k, shape, fan_in):
        return jax.random.normal(k, shape, F32) * (fan_in ** -0.5)

    def gain(k, shape):
        return 1.0 + 0.02 * jax.random.normal(k, shape, F32)

    return {
        "x": jax.random.normal(ks[0], (BATCH, SEQ, D), F32),
        "attn_norm": gain(ks[1], (DEPTH, D)),
        "ffn_norm": gain(ks[2], (DEPTH, D)),
        "final_norm": gain(ks[3], (D,)),
        "moba_w_qkv": w(ks[4], (N_MOBA, D, 3 * AW), D),
        "moba_w_o": w(ks[5], (N_MOBA, AW, D), AW),
        "mla_w_down": w(ks[6], (N_MLA, D, MLA_Q_RANK + MLA_KV_RANK + MLA_ROPE_DIM), D),
        "mla_q_norm": gain(ks[7], (N_MLA, MLA_Q_RANK)),
        "mla_kv_norm": gain(ks[8], (N_MLA, MLA_KV_RANK)),
        "mla_w_uq": w(ks[9], (N_MLA, MLA_Q_RANK, MLA_HEADS * (MLA_NOPE_DIM + MLA_ROPE_DIM)), MLA_Q_RANK),
        "mla_w_ukv": w(ks[10], (N_MLA, MLA_KV_RANK, MLA_HEADS * (MLA_NOPE_DIM + MLA_V_DIM)), MLA_KV_RANK),
        "mla_w_o": w(ks[11], (N_MLA, MLA_HEADS * MLA_V_DIM, D), MLA_HEADS * MLA_V_DIM),
        "dil_w_qkv": w(ks[12], (N_DIL, D, 3 * AW), D),
        "dil_w_o": w(ks[13], (N_DIL, AW, D), AW),
        "ffn_w_up": w(ks[14], (DEPTH, D, 2 * D_FF), D),
        "ffn_conv_w": w(ks[15], (DEPTH, CONV_WIDTH, 2 * D_FF), CONV_WIDTH),
        "ffn_conv_b": 0.02 * jax.random.normal(ks[16], (DEPTH, 2 * D_FF), F32),
        "ffn_w_down": w(ks[17], (DEPTH, D_FF, D), D_FF),
    }


def reference(x, attn_norm, ffn_norm, final_norm,
              moba_w_qkv, moba_w_o,
              mla_w_down, mla_q_norm, mla_kv_norm, mla_w_uq, mla_w_ukv, mla_w_o,
              dil_w_qkv, dil_w_o,
              ffn_w_up, ffn_conv_w, ffn_conv_b, ffn_w_down):
    S = x.shape[1]
    cos, sin = rope_tables(S, HEAD_DIM)
    cos_r, sin_r = rope_tables(S, MLA_ROPE_DIM)
    h = x
    for i in range(DEPTH):
        j = i // N_MIXERS
        kind = i % N_MIXERS
        xn = rms_norm(h, attn_norm[i])
        if kind == 0:
            mix = moba_attention(xn, moba_w_qkv[j], moba_w_o[j], cos, sin)
        elif kind == 1:
            mix = mla_attention(xn, mla_w_down[j], mla_q_norm[j], mla_kv_norm[j],
                                mla_w_uq[j], mla_w_ukv[j], mla_w_o[j], cos_r, sin_r)
        else:
            mix = dilated_attention(xn, dil_w_qkv[j], dil_w_o[j], cos, sin)
        h = h + mix
        h = h + conv_glu_ffn(rms_norm(h, ffn_norm[i]), ffn_w_up[i], ffn_conv_w[i],
                             ffn_conv_b[i], ffn_w_down[i])
    return rms_norm(h, final_norm)
```

```python
import functools
import math

import jax
import jax.numpy as jnp
from jax import lax
from jax.experimental import pallas as pl
from jax.experimental.pallas import tpu as pltpu

F32 = jnp.float32
BF16 = jnp.bfloat16

D_MODEL = 2048
DEPTH = 4
N_MIXERS = 3
ROPE_THETA = 10000.0
NORM_EPS = 1e-6

N_HEADS = 16
HEAD_DIM = D_MODEL // N_HEADS
MOBA_BLOCK = 256
MOBA_TOPK = 3

MLA_HEADS = 16
MLA_Q_RANK = 512
MLA_KV_RANK = 512
MLA_NOPE_DIM = 128
MLA_ROPE_DIM = 64
MLA_V_DIM = 128
MLA_QK_PAD = 256

DIL_PATTERNS = ((128, 1), (512, 4), (2048, 16))
DIL_BAND = 128

D_FF = 5632
CONV_WIDTH = 3

LANES = 128
HALO_ROWS = 16
VMEM_LIMIT_BYTES = 56 * 1024 * 1024

NEG_INF = float("-inf")


def _params(*semantics):
    return pltpu.CompilerParams(dimension_semantics=semantics,
                                vmem_limit_bytes=VMEM_LIMIT_BYTES)


def _rms_normalize(x, gain):
    ms = jnp.mean(x * x, axis=-1, keepdims=True)
    return x * lax.rsqrt(ms + NORM_EPS) * gain


def _norm_matmul_kernel(x_ref, g_ref, w_ref, cos_ref, sin_ref, o_ref, xn_ref, *, rope_tiles):
    j = pl.program_id(1)

    @pl.when(j == 0)
    def _():
        xn_ref[...] = _rms_normalize(x_ref[...], g_ref[...]).astype(BF16)

    acc = jnp.dot(xn_ref[...], w_ref[...], preferred_element_type=F32)

    @pl.when(j < rope_tiles)
    def _():
        cos = cos_ref[...]
        sin = sin_ref[...]
        for c in range(acc.shape[1] // HEAD_DIM):
            xh = acc[:, c * HEAD_DIM:(c + 1) * HEAD_DIM]
            roped = xh * cos + pltpu.roll(xh, HEAD_DIM // 2, axis=1) * sin
            o_ref[:, c * HEAD_DIM:(c + 1) * HEAD_DIM] = roped.astype(o_ref.dtype)

    @pl.when(j >= rope_tiles)
    def _():
        o_ref[...] = acc.astype(o_ref.dtype)


def _norm_matmul(x, gain, w, cos_full, sin_signed, *, rope_cols, tm=1024, tn=1024):
    m, k = x.shape
    n = w.shape[1]
    assert m % tm == 0 and n % tn == 0 and rope_cols % tn == 0
    kern = functools.partial(_norm_matmul_kernel, rope_tiles=rope_cols // tn)
    return pl.pallas_call(
        kern,
        grid=(m // tm, n // tn),
        in_specs=[
            pl.BlockSpec((tm, k), lambda i, j: (i, 0)),
            pl.BlockSpec((1, k), lambda i, j: (0, 0)),
            pl.BlockSpec((k, tn), lambda i, j: (0, j)),
            pl.BlockSpec((tm, HEAD_DIM), lambda i, j: (i, 0)),
            pl.BlockSpec((tm, HEAD_DIM), lambda i, j: (i, 0)),
        ],
        out_specs=pl.BlockSpec((tm, tn), lambda i, j: (i, j)),
        out_shape=jax.ShapeDtypeStruct((m, n), BF16),
        scratch_shapes=[pltpu.VMEM((tm, k), BF16)],
        compiler_params=_params("parallel", "arbitrary"),
        name="norm_matmul",
    )(x, gain.reshape(1, k), w, cos_full, sin_signed)


def _matmul_residual_kernel(a_ref, w_ref, r_ref, o_ref):
    o_ref[...] = r_ref[...] + jnp.dot(a_ref[...], w_ref[...], preferred_element_type=F32)


def _matmul_residual(a, w, res, *, tm=1024, tn=512):
    m, k = a.shape
    n = w.shape[1]
    assert m % tm == 0 and n % tn == 0
    return pl.pallas_call(
        _matmul_residual_kernel,
        grid=(m // tm, n // tn),
        in_specs=[
            pl.BlockSpec((tm, k), lambda i, j: (i, 0)),
            pl.BlockSpec((k, tn), lambda i, j: (0, j)),
            pl.BlockSpec((tm, tn), lambda i, j: (i, j)),
        ],
        out_specs=pl.BlockSpec((tm, tn), lambda i, j: (i, j)),
        out_shape=jax.ShapeDtypeStruct((m, n), F32),
        compiler_params=_params("parallel", "arbitrary"),
        name="matmul_residual",
    )(a, w, res)


def _ffn_up_kernel(x_ref, halo_ref, g_ref, wg_ref, wv_ref, cwg_ref, cwv_ref, cbg_ref, cbv_ref,
                   o_ref, xn_ref):
    i = pl.program_id(0)
    j = pl.program_id(1)
    tm = x_ref.shape[0]

    @pl.when(j == 0)
    def _():
        gain = g_ref[...]
        halo = _rms_normalize(halo_ref[...], gain)
        halo = jnp.where(i == 0, 0.0, halo)
        xn_ref[0:HALO_ROWS, :] = halo.astype(BF16)
        xn_ref[HALO_ROWS:, :] = _rms_normalize(x_ref[...], gain).astype(BF16)

    def conv(w_ref, cw_ref, cb_ref):
        h = jnp.dot(xn_ref[...], w_ref[...], preferred_element_type=F32)
        cw = cw_ref[...]
        y = (pltpu.roll(h, 2, axis=0) * cw[0:1, :] + pltpu.roll(h, 1, axis=0) * cw[1:2, :]
             + h * cw[2:3, :])
        return y[HALO_ROWS:, :] + cb_ref[...]

    gate = conv(wg_ref, cwg_ref, cbg_ref)
    val = conv(wv_ref, cwv_ref, cbv_ref)
    o_ref[...] = (gate * (1.0 / (1.0 + jnp.exp(-gate))) * val).astype(o_ref.dtype)


def _ffn_up(x, gain, w_up, conv_w, conv_b, *, tm=1024, tn=512):
    m, k = x.shape
    assert m % tm == 0 and D_FF % tn == 0 and tm % HALO_ROWS == 0
    nj = D_FF // tn
    halo_blocks = tm // HALO_ROWS
    cb = conv_b.reshape(1, 2 * D_FF)
    return pl.pallas_call(
        _ffn_up_kernel,
        grid=(m // tm, nj),
        in_specs=[
            pl.BlockSpec((tm, k), lambda i, j: (i, 0)),
            pl.BlockSpec((HALO_ROWS, k), lambda i, j: (jnp.maximum(i * halo_blocks - 1, 0), 0)),
            pl.BlockSpec((1, k), lambda i, j: (0, 0)),
            pl.BlockSpec((k, tn), lambda i, j: (0, j)),
            pl.BlockSpec((k, tn), lambda i, j: (0, nj + j)),
            pl.BlockSpec((CONV_WIDTH, tn), lambda i, j: (0, j)),
            pl.BlockSpec((CONV_WIDTH, tn), lambda i, j: (0, nj + j)),
            pl.BlockSpec((1, tn), lambda i, j: (0, j)),
            pl.BlockSpec((1, tn), lambda i, j: (0, nj + j)),
        ],
        out_specs=pl.BlockSpec((tm, tn), lambda i, j: (i, j)),
        out_shape=jax.ShapeDtypeStruct((m, D_FF), BF16),
        scratch_shapes=[pltpu.VMEM((HALO_ROWS + tm, k), BF16)],
        compiler_params=_params("parallel", "arbitrary"),
        name="ffn_up",
    )(x, x, gain.reshape(1, k), w_up, w_up, conv_w, conv_w, cb, cb)


def _scores(q, k, scale):
    return lax.dot_general(q, k, (((1,), (1,)), ((), ())), preferred_element_type=F32) * scale


def _softmax_init(s, v, m_ref, l_ref, acc_ref):
    m = jnp.max(s, axis=1, keepdims=True)
    p = jnp.exp(s - m)
    m_ref[...] = jnp.broadcast_to(m, m_ref.shape)
    l_ref[...] = jnp.broadcast_to(jnp.sum(p, axis=1, keepdims=True), l_ref.shape)
    acc_ref[...] = jnp.dot(p.astype(BF16), v, preferred_element_type=F32)


def _softmax_update(s, v, m_ref, l_ref, acc_ref):
    m_prev = m_ref[...]
    m_new = jnp.maximum(m_prev, jnp.max(s, axis=1, keepdims=True))
    alpha = jnp.exp(m_prev - m_new)
    p = jnp.exp(s - jnp.tile(m_new, (1, s.shape[1] // LANES)))
    l_ref[...] = alpha * l_ref[...] + jnp.sum(p, axis=1, keepdims=True)
    m_ref[...] = m_new
    acc_ref[...] = alpha * acc_ref[...] + jnp.dot(p.astype(BF16), v, preferred_element_type=F32)


def _moba_kernel(q_ref, k_ref, v_ref, o_ref, kmh_ref, kml_ref, m_ref, l_ref, acc_ref, *, scale):
    i = pl.program_id(1)
    bs = MOBA_BLOCK
    nb = k_ref.shape[0] // bs

    @pl.when(i == 0)
    def _():
        for b in range(nb):
            km = jnp.mean(k_ref[b * bs:(b + 1) * bs, :].astype(F32), axis=0, keepdims=True)
            hi = km.astype(BF16).astype(F32)
            kmh_ref[b:b + 1, :] = hi
            kml_ref[b:b + 1, :] = km - hi

    q = q_ref[...]
    nt = (((1,), (1,)), ((), ()))
    gate = (lax.dot_general(q, kmh_ref[...].astype(BF16), nt, preferred_element_type=F32)
            + lax.dot_general(q, kml_ref[...].astype(BF16), nt, preferred_element_type=F32))
    blk = lax.broadcasted_iota(jnp.int32, gate.shape, 1).astype(F32)
    n_past = i.astype(F32)
    gate = jnp.where(blk < n_past, gate, NEG_INF)
    sel = jnp.zeros(gate.shape, F32)
    for r in range(min(MOBA_TOPK, nb)):
        mx = jnp.max(gate, axis=1, keepdims=True)
        first = jnp.min(jnp.where(gate == mx, blk, float(nb)), axis=1, keepdims=True)
        pick = blk == first
        sel = jnp.maximum(sel, jnp.where(pick, jnp.where(r < i, 1.0, 0.0), 0.0))
        gate = jnp.where(pick, NEG_INF, gate)

    row = lax.broadcasted_iota(jnp.int32, (bs, bs), 0)
    col = lax.broadcasted_iota(jnp.int32, (bs, bs), 1)
    own = pl.multiple_of(i * bs, bs)
    s = _scores(q, k_ref[pl.ds(own, bs), :], scale)
    s = jnp.where(col <= row, s, NEG_INF)
    _softmax_init(s, v_ref[pl.ds(own, bs), :], m_ref, l_ref, acc_ref)

    def body(j, carry):
        start = pl.multiple_of(j * bs, bs)
        chosen = jnp.sum(jnp.where(blk == j.astype(F32), sel, 0.0), axis=1, keepdims=True)
        bias = jnp.where(chosen > 0.5, 0.0, NEG_INF)
        s = _scores(q, k_ref[pl.ds(start, bs), :], scale) + bias
        _softmax_update(s, v_ref[pl.ds(start, bs), :], m_ref, l_ref, acc_ref)
        return carry

    lax.fori_loop(0, i, body, 0)
    o_ref[...] = (acc_ref[...] / l_ref[...]).astype(o_ref.dtype)


def _moba_attention(qkv):
    s = qkv.shape[0]
    bs = MOBA_BLOCK
    assert s % bs == 0
    nb = s // bs
    kern = functools.partial(_moba_kernel, scale=HEAD_DIM ** -0.5)
    return pl.pallas_call(
        kern,
        grid=(N_HEADS, nb),
        in_specs=[
            pl.BlockSpec((bs, HEAD_DIM), lambda h, i: (i, h)),
            pl.BlockSpec((s, HEAD_DIM), lambda h, i: (0, N_HEADS + h)),
            pl.BlockSpec((s, HEAD_DIM), lambda h, i: (0, 2 * N_HEADS + h)),
        ],
        out_specs=pl.BlockSpec((bs, HEAD_DIM), lambda h, i: (i, h)),
        out_shape=jax.ShapeDtypeStruct((s, N_HEADS * HEAD_DIM), BF16),
        scratch_shapes=[
            pltpu.VMEM((nb, HEAD_DIM), F32),
            pltpu.VMEM((nb, HEAD_DIM), F32),
            pltpu.VMEM((bs, LANES), F32),
            pltpu.VMEM((bs, LANES), F32),
            pltpu.VMEM((bs, HEAD_DIM), F32),
        ],
        compiler_params=_params("parallel", "arbitrary"),
        name="moba_attention",
    )(qkv, qkv, qkv)


def _mla_down_kernel(x_ref, g_ref, w_ref, qn_ref, kvn_ref, cos_ref, sin_ref,
                     cq_ref, ckv_ref, kr_ref):
    xn = _rms_normalize(x_ref[...], g_ref[...]).astype(BF16)
    lat = jnp.dot(xn, w_ref[...], preferred_element_type=F32)
    q0, kv0, r0, rr0 = 0, MLA_Q_RANK, MLA_Q_RANK + MLA_KV_RANK, MLA_Q_RANK + MLA_KV_RANK + LANES
    cq_ref[...] = _rms_normalize(lat[:, q0:kv0], qn_ref[...]).astype(BF16)
    ckv_ref[...] = _rms_normalize(lat[:, kv0:r0], kvn_ref[...]).astype(BF16)
    kr_ref[...] = (lat[:, r0:rr0] * cos_ref[...] + lat[:, rr0:rr0 + LANES] * sin_ref[...]).astype(BF16)


def _mla_down(x, gain, w_ext, q_norm, kv_norm, cos_r, sin_r, *, tm=512):
    m, k = x.shape
    n = w_ext.shape[1]
    assert m % tm == 0
    row = lambda i: (i, 0)
    fixed = lambda i: (0, 0)
    return pl.pallas_call(
        _mla_down_kernel,
        grid=(m // tm,),
        in_specs=[
            pl.BlockSpec((tm, k), row),
            pl.BlockSpec((1, k), fixed),
            pl.BlockSpec((k, n), fixed),
            pl.BlockSpec((1, MLA_Q_RANK), fixed),
            pl.BlockSpec((1, MLA_KV_RANK), fixed),
            pl.BlockSpec((tm, LANES), row),
            pl.BlockSpec((tm, LANES), row),
        ],
        out_specs=[
            pl.BlockSpec((tm, MLA_Q_RANK), row),
            pl.BlockSpec((tm, MLA_KV_RANK), row),
            pl.BlockSpec((tm, LANES), row),
        ],
        out_shape=[
            jax.ShapeDtypeStruct((m, MLA_Q_RANK), BF16),
            jax.ShapeDtypeStruct((m, MLA_KV_RANK), BF16),
            jax.ShapeDtypeStruct((m, LANES), BF16),
        ],
        compiler_params=_params("parallel"),
        name="mla_down",
    )(x, gain.reshape(1, k), w_ext, q_norm.reshape(1, -1), kv_norm.reshape(1, -1), cos_r, sin_r)


def _mla_up_kernel(cq_ref, ckv_ref, wq_ref, wkv_ref, cos_ref, sin_ref, q_ref, kv_ref):
    cq = cq_ref[...]
    cos = cos_ref[...]
    sin = sin_ref[...]
    per_head = MLA_NOPE_DIM + 2 * LANES
    for h in range(MLA_HEADS):
        r = jnp.dot(cq, wq_ref[:, h * per_head:(h + 1) * per_head], preferred_element_type=F32)
        rope = r[:, MLA_NOPE_DIM:MLA_NOPE_DIM + LANES] * cos + r[:, MLA_NOPE_DIM + LANES:] * sin
        q_ref[:, h * MLA_QK_PAD:h * MLA_QK_PAD + MLA_NOPE_DIM] = r[:, :MLA_NOPE_DIM].astype(BF16)
        q_ref[:, h * MLA_QK_PAD + MLA_NOPE_DIM:(h + 1) * MLA_QK_PAD] = rope.astype(BF16)
    kv_ref[...] = jnp.dot(ckv_ref[...], wkv_ref[...], preferred_element_type=F32).astype(BF16)


def _mla_up(cq, ckv, wq_ext, wkv, cos_r, sin_r, *, tm=512):
    m = cq.shape[0]
    assert m % tm == 0
    row = lambda i: (i, 0)
    fixed = lambda i: (0, 0)
    nq = MLA_HEADS * MLA_QK_PAD
    nkv = wkv.shape[1]
    return pl.pallas_call(
        _mla_up_kernel,
        grid=(m // tm,),
        in_specs=[
            pl.BlockSpec((tm, MLA_Q_RANK), row),
            pl.BlockSpec((tm, MLA_KV_RANK), row),
            pl.BlockSpec(wq_ext.shape, fixed),
            pl.BlockSpec(wkv.shape, fixed),
            pl.BlockSpec((tm, LANES), row),
            pl.BlockSpec((tm, LANES), row),
        ],
        out_specs=[pl.BlockSpec((tm, nq), row), pl.BlockSpec((tm, nkv), row)],
        out_shape=[jax.ShapeDtypeStruct((m, nq), BF16), jax.ShapeDtypeStruct((m, nkv), BF16)],
        compiler_params=_params("parallel"),
        name="mla_up",
    )(cq, ckv, wq_ext, wkv, cos_r, sin_r)


def _mla_attn_kernel(q_ref, kn_ref, v_ref, kr_ref, o_ref, m_ref, l_ref, acc_ref, *, scale, tq):
    i = pl.program_id(1)
    q = q_ref[...]

    def keys(start):
        return jnp.concatenate([kn_ref[pl.ds(start, tq), :], kr_ref[pl.ds(start, tq), :]], axis=1)

    row = lax.broadcasted_iota(jnp.int32, (tq, tq), 0)
    col = lax.broadcasted_iota(jnp.int32, (tq, tq), 1)
    own = pl.multiple_of(i * tq, tq)
    s = jnp.where(col <= row, _scores(q, keys(own), scale), NEG_INF)
    _softmax_init(s, v_ref[pl.ds(own, tq), :], m_ref, l_ref, acc_ref)

    def body(j, carry):
        start = pl.multiple_of(j * tq, tq)
        _softmax_update(_scores(q, keys(start), scale), v_ref[pl.ds(start, tq), :],
                        m_ref, l_ref, acc_ref)
        return carry

    lax.fori_loop(0, i, body, 0)
    o_ref[...] = (acc_ref[...] / l_ref[...]).astype(o_ref.dtype)


def _mla_attention(q_full, kv, k_rope, *, tq=256):
    s = q_full.shape[0]
    assert s % tq == 0
    kern = functools.partial(_mla_attn_kernel, tq=tq,
                             scale=(MLA_NOPE_DIM + MLA_ROPE_DIM) ** -0.5)
    return pl.pallas_call(
        kern,
        grid=(MLA_HEADS, s // tq),
        in_specs=[
            pl.BlockSpec((tq, MLA_QK_PAD), lambda h, i: (i, h)),
            pl.BlockSpec((s, MLA_NOPE_DIM), lambda h, i: (0, h)),
            pl.BlockSpec((s, MLA_V_DIM), lambda h, i: (0, MLA_HEADS + h)),
            pl.BlockSpec((s, LANES), lambda h, i: (0, 0)),
        ],
        out_specs=pl.BlockSpec((tq, MLA_V_DIM), lambda h, i: (i, h)),
        out_shape=jax.ShapeDtypeStruct((s, MLA_HEADS * MLA_V_DIM), BF16),
        scratch_shapes=[
            pltpu.VMEM((tq, LANES), F32),
            pltpu.VMEM((tq, LANES), F32),
            pltpu.VMEM((tq, MLA_V_DIM), F32),
        ],
        compiler_params=_params("parallel", "arbitrary"),
        name="mla_attention",
    )(q_full, kv, kv, k_rope)


def _dilated_kernel(q_ref, k_ref, v_ref, o_ref, lse_ref, *, scale):
    band = DIL_BAND
    nb = q_ref.shape[0] // band
    row = lax.broadcasted_iota(jnp.int32, (band, 2 * band), 0)
    col = lax.broadcasted_iota(jnp.int32, (band, 2 * band), 1)
    outside = jnp.where(col < band, row - col, col - band - row)
    bias_inner = jnp.where(outside <= 0, 0.0, NEG_INF)
    bias_first = jnp.where(col <= row, 0.0, NEG_INF)

    def body(n, carry):
        qs = pl.multiple_of(n * band, band)
        ks = pl.multiple_of(jnp.maximum(n - 1, 0) * band, band)
        s = _scores(q_ref[pl.ds(qs, band), :], k_ref[pl.ds(ks, 2 * band), :], scale)
        s = s + jnp.where(n == 0, bias_first, bias_inner)
        m = jnp.max(s, axis=1, keepdims=True)
        p = jnp.exp(s - m)
        l = jnp.sum(p, axis=1, keepdims=True)
        o = jnp.dot(p.astype(BF16), v_ref[pl.ds(ks, 2 * band), :], preferred_element_type=F32)
        o_ref[pl.ds(qs, band), :] = o / l
        lse_ref[pl.ds(qs, band), :] = jnp.broadcast_to(m + jnp.log(l), (band, HEAD_DIM))
        return carry

    lax.fori_loop(0, nb, body, 0)


def _dilated_pattern(qkv, dil):
    s = qkv.shape[0]
    rows = s // dil
    assert s % dil == 0 and rows % DIL_BAND == 0 and rows >= 2 * DIL_BAND
    view = qkv.reshape(rows, dil * qkv.shape[1])
    per_res = qkv.shape[1] // HEAD_DIM
    kern = functools.partial(_dilated_kernel, scale=HEAD_DIM ** -0.5)
    out_sds = jax.ShapeDtypeStruct((rows, dil * N_HEADS * HEAD_DIM), F32)
    out_spec = pl.BlockSpec((rows, HEAD_DIM), lambda h, r: (0, r * N_HEADS + h))
    o, lse = pl.pallas_call(
        kern,
        grid=(N_HEADS, dil),
        in_specs=[
            pl.BlockSpec((rows, HEAD_DIM), lambda h, r: (0, r * per_res + h)),
            pl.BlockSpec((rows, HEAD_DIM), lambda h, r: (0, r * per_res + N_HEADS + h)),
            pl.BlockSpec((rows, HEAD_DIM), lambda h, r: (0, r * per_res + 2 * N_HEADS + h)),
        ],
        out_specs=[out_spec, out_spec],
        out_shape=[out_sds, out_sds],
        compiler_params=_params("parallel", "parallel"),
        name=f"dilated_d{dil}",
    )(view, view, view)
    return o.reshape(s, N_HEADS * HEAD_DIM), lse.reshape(s, N_HEADS * HEAD_DIM)


def _dil_merge_kernel(o1, o2, o3, l1, l2, l3, out_ref):
    a, b, c = l1[...], l2[...], l3[...]
    mx = jnp.maximum(jnp.maximum(a, b), c)
    ea, eb, ec = jnp.exp(a - mx), jnp.exp(b - mx), jnp.exp(c - mx)
    num = ea * o1[...] + eb * o2[...] + ec * o3[...]
    out_ref[...] = (num / (ea + eb + ec)).astype(out_ref.dtype)


def _dilated_attention(qkv, *, tm=512):
    outs, lses = zip(*[_dilated_pattern(qkv, dil) for _, dil in DIL_PATTERNS])
    s, n = outs[0].shape
    assert s % tm == 0
    spec = pl.BlockSpec((tm, n), lambda i: (i, 0))
    return pl.pallas_call(
        _dil_merge_kernel,
        grid=(s // tm,),
        in_specs=[spec] * 6,
        out_specs=spec,
        out_shape=jax.ShapeDtypeStruct((s, n), BF16),
        compiler_params=_params("parallel"),
        name="dilated_merge",
    )(*outs, *lses)


def _final_norm_kernel(x_ref, g_ref, o_ref):
    o_ref[...] = _rms_normalize(x_ref[...], g_ref[...])


def _final_norm(x, gain, *, tm=512):
    m, k = x.shape
    assert m % tm == 0
    return pl.pallas_call(
        _final_norm_kernel,
        grid=(m // tm,),
        in_specs=[pl.BlockSpec((tm, k), lambda i: (i, 0)), pl.BlockSpec((1, k), lambda i: (0, 0))],
        out_specs=pl.BlockSpec((tm, k), lambda i: (i, 0)),
        out_shape=jax.ShapeDtypeStruct((m, k), F32),
        compiler_params=_params("parallel"),
        name="final_norm",
    )(x, gain.reshape(1, k))


def _rope_tables(n_pos, dim):
    inv = 1.0 / (ROPE_THETA ** (jnp.arange(0, dim, 2, dtype=F32) / dim))
    ang = jnp.arange(n_pos, dtype=F32)[:, None] * inv[None, :]
    return jnp.cos(ang), jnp.sin(ang)


def _rotate_half_columns(w):
    half = w.shape[1] // 2
    return jnp.concatenate([-w[:, half:], w[:, :half]], axis=1)


def _pad_cols(w, width):
    return jnp.pad(w, ((0, 0), (0, width - w.shape[1])))


def _mla_weights(w_down, w_uq, w_ukv):
    lat = MLA_Q_RANK + MLA_KV_RANK
    w_rope = w_down[:, lat:]
    w_down_ext = jnp.concatenate(
        [w_down[:, :lat], _pad_cols(w_rope, LANES), _pad_cols(_rotate_half_columns(w_rope), LANES)],
        axis=1).astype(BF16)
    wq = w_uq.reshape(MLA_Q_RANK, MLA_HEADS, MLA_NOPE_DIM + MLA_ROPE_DIM)
    heads = []
    for h in range(MLA_HEADS):
        w_r = wq[:, h, MLA_NOPE_DIM:]
        heads += [wq[:, h, :MLA_NOPE_DIM], _pad_cols(w_r, LANES),
                  _pad_cols(_rotate_half_columns(w_r), LANES)]
    wq_ext = jnp.concatenate(heads, axis=1).astype(BF16)
    wkv = w_ukv.reshape(MLA_KV_RANK, MLA_HEADS, MLA_NOPE_DIM + MLA_V_DIM)
    wkv = jnp.concatenate([wkv[:, :, :MLA_NOPE_DIM].reshape(MLA_KV_RANK, -1),
                           wkv[:, :, MLA_NOPE_DIM:].reshape(MLA_KV_RANK, -1)], axis=1).astype(BF16)
    return w_down_ext, wq_ext, wkv


def kernel(x, attn_norm, ffn_norm, final_norm, moba_w_qkv, moba_w_o, mla_w_down, mla_q_norm,
           mla_kv_norm, mla_w_uq, mla_w_ukv, mla_w_o, dil_w_qkv, dil_w_o, ffn_w_up, ffn_conv_w,
           ffn_conv_b, ffn_w_down):
    b, s, d = x.shape
    assert b == 1 and d == D_MODEL
    cos, sin = _rope_tables(s, HEAD_DIM)
    cos_full = jnp.concatenate([cos, cos], axis=1)
    sin_signed = jnp.concatenate([-sin, sin], axis=1)
    cos_r, sin_r = _rope_tables(s, MLA_ROPE_DIM)
    cos_r = _pad_cols(jnp.concatenate([cos_r, cos_r], axis=1), LANES)
    sin_r = _pad_cols(jnp.concatenate([sin_r, sin_r], axis=1), LANES)
    attn_width = N_HEADS * HEAD_DIM

    h = x.reshape(s, d)
    for i in range(DEPTH):
        j = i // N_MIXERS
        kind = i % N_MIXERS
        if kind == 0:
            qkv = _norm_matmul(h, attn_norm[i], moba_w_qkv[j].astype(BF16), cos_full, sin_signed,
                               rope_cols=2 * attn_width)
            mix = _moba_attention(qkv)
            w_o = moba_w_o[j]
        elif kind == 1:
            w_down_ext, wq_ext, wkv = _mla_weights(mla_w_down[j], mla_w_uq[j], mla_w_ukv[j])
            cq, ckv, k_rope = _mla_down(h, attn_norm[i], w_down_ext, mla_q_norm[j], mla_kv_norm[j],
                                        cos_r, sin_r)
            q_full, kv = _mla_up(cq, ckv, wq_ext, wkv, cos_r, sin_r)
            mix = _mla_attention(q_full, kv, k_rope)
            w_o = mla_w_o[j]
        else:
            qkv = _norm_matmul(h, attn_norm[i], dil_w_qkv[j].astype(BF16), cos_full, sin_signed,
                               rope_cols=2 * attn_width)
            mix = _dilated_attention(qkv)
            w_o = dil_w_o[j]
        h = _matmul_residual(mix, w_o.astype(BF16), h, tn=1024)
        act = _ffn_up(h, ffn_norm[i], ffn_w_up[i].astype(BF16), ffn_conv_w[i], ffn_conv_b[i])
        h = _matmul_residual(act, ffn_w_down[i].astype(BF16), h, tn=512)
    return _final_norm(h, final_norm).reshape(b, s, d)
```

```python
import functools
import math

import jax
import jax.numpy as jnp
from jax import lax
from jax.experimental import pallas as pl
from jax.experimental.pallas import tpu as pltpu

F32 = jnp.float32
BF16 = jnp.bfloat16

D_MODEL = 2048
DEPTH = 4
N_MIXERS = 3
ROPE_THETA = 10000.0
NORM_EPS = 1e-6

N_HEADS = 16
HEAD_DIM = D_MODEL // N_HEADS
MOBA_BLOCK = 256
MOBA_TOPK = 3

MLA_HEADS = 16
MLA_Q_RANK = 512
MLA_KV_RANK = 512
MLA_NOPE_DIM = 128
MLA_ROPE_DIM = 64
MLA_V_DIM = 128

DIL_PATTERNS = ((128, 1), (512, 4), (2048, 16))
DIL_BAND = 128
DIL_UNROLL = 4

D_FF = 5632
CONV_WIDTH = 3

LANES = 128
HALO_ROWS = 16
VMEM_LIMIT_BYTES = 56 * 1024 * 1024
FLASH_TILE = 512

LOG2E = math.log2(math.e)
NEG_INF = float("-inf")
MASKED = -1e30

_NT = (((1,), (1,)), ((), ()))


def _params(*semantics):
    return pltpu.CompilerParams(dimension_semantics=semantics,
                                vmem_limit_bytes=VMEM_LIMIT_BYTES)


def _rms_normalize(x, gain):
    ms = jnp.mean(x * x, axis=-1, keepdims=True)
    return x * lax.rsqrt(ms + NORM_EPS) * gain


def _norm_matmul_kernel(x_ref, g_ref, w_ref, cos_ref, sin_ref, o_ref, xn_ref, *,
                        q_tiles, rope_tiles, q_scale):
    j = pl.program_id(1)

    @pl.when(j == 0)
    def _():
        xn_ref[...] = _rms_normalize(x_ref[...], g_ref[...]).astype(BF16)

    acc = jnp.dot(xn_ref[...], w_ref[...], preferred_element_type=F32)

    @pl.when(j < rope_tiles)
    def _():
        col_scale = jnp.where(j < q_tiles, q_scale, 1.0)
        cos = cos_ref[...] * col_scale
        sin = sin_ref[...] * col_scale
        for c in range(acc.shape[1] // HEAD_DIM):
            xh = acc[:, c * HEAD_DIM:(c + 1) * HEAD_DIM]
            roped = xh * cos + pltpu.roll(xh, HEAD_DIM // 2, axis=1) * sin
            o_ref[:, c * HEAD_DIM:(c + 1) * HEAD_DIM] = roped.astype(o_ref.dtype)

    @pl.when(j >= rope_tiles)
    def _():
        o_ref[...] = acc.astype(o_ref.dtype)


def _qkv_projection(x, gain, w, cos_full, sin_signed, *, tm=1024, tn=1024):
    m, k = x.shape
    n = w.shape[1]
    width = N_HEADS * HEAD_DIM
    assert m % tm == 0 and n % tn == 0 and width % tn == 0 and n == 3 * width
    kern = functools.partial(_norm_matmul_kernel, q_tiles=width // tn, rope_tiles=2 * width // tn,
                             q_scale=HEAD_DIM ** -0.5 * LOG2E)
    return pl.pallas_call(
        kern,
        grid=(m // tm, n // tn),
        in_specs=[
            pl.BlockSpec((tm, k), lambda i, j: (i, 0)),
            pl.BlockSpec((1, k), lambda i, j: (0, 0)),
            pl.BlockSpec((k, tn), lambda i, j: (0, j)),
            pl.BlockSpec((tm, HEAD_DIM), lambda i, j: (i, 0)),
            pl.BlockSpec((tm, HEAD_DIM), lambda i, j: (i, 0)),
        ],
        out_specs=pl.BlockSpec((tm, tn), lambda i, j: (i, j)),
        out_shape=jax.ShapeDtypeStruct((m, n), BF16),
        scratch_shapes=[pltpu.VMEM((tm, k), BF16)],
        compiler_params=_params("parallel", "arbitrary"),
        name="qkv_projection",
    )(x, gain.reshape(1, k), w, cos_full, sin_signed)


def _matmul_residual_kernel(a_ref, w_ref, r_ref, o_ref):
    o_ref[...] = r_ref[...] + jnp.dot(a_ref[...], w_ref[...], preferred_element_type=F32)


def _matmul_residual(a, w, res, *, tm=1024, tn=512):
    m, k = a.shape
    n = w.shape[1]
    assert m % tm == 0 and n % tn == 0
    return pl.pallas_call(
        _matmul_residual_kernel,
        grid=(m // tm, n // tn),
        in_specs=[
            pl.BlockSpec((tm, k), lambda i, j: (i, 0)),
            pl.BlockSpec((k, tn), lambda i, j: (0, j)),
            pl.BlockSpec((tm, tn), lambda i, j: (i, j)),
        ],
        out_specs=pl.BlockSpec((tm, tn), lambda i, j: (i, j)),
        out_shape=jax.ShapeDtypeStruct((m, n), F32),
        compiler_params=_params("parallel", "arbitrary"),
        name="matmul_residual",
    )(a, w, res)


def _ffn_up_kernel(x_ref, halo_ref, g_ref, wg_ref, wv_ref, cwg_ref, cwv_ref, cbg_ref, cbv_ref,
                   o_ref, xn_ref):
    i = pl.program_id(0)
    j = pl.program_id(1)

    @pl.when(j == 0)
    def _():
        gain = g_ref[...]
        halo = _rms_normalize(halo_ref[...], gain)
        halo = jnp.where(i == 0, 0.0, halo)
        xn_ref[0:HALO_ROWS, :] = halo.astype(BF16)
        xn_ref[HALO_ROWS:, :] = _rms_normalize(x_ref[...], gain).astype(BF16)

    def conv(w_ref, cw_ref, cb_ref):
        h = jnp.dot(xn_ref[...], w_ref[...], preferred_element_type=F32)
        cw = cw_ref[...]
        y = (pltpu.roll(h, 2, axis=0) * cw[0:1, :] + pltpu.roll(h, 1, axis=0) * cw[1:2, :]
             + h * cw[2:3, :])
        return y[HALO_ROWS:, :] + cb_ref[...]

    gate = conv(wg_ref, cwg_ref, cbg_ref)
    val = conv(wv_ref, cwv_ref, cbv_ref)
    o_ref[...] = (gate * (1.0 / (1.0 + jnp.exp(-gate))) * val).astype(o_ref.dtype)


def _ffn_up(x, gain, w_up, conv_w, conv_b, *, tm=1024, tn=512):
    m, k = x.shape
    assert m % tm == 0 and D_FF % tn == 0 and tm % HALO_ROWS == 0
    nj = D_FF // tn
    halo_blocks = tm // HALO_ROWS
    cb = conv_b.reshape(1, 2 * D_FF)
    return pl.pallas_call(
        _ffn_up_kernel,
        grid=(m // tm, nj),
        in_specs=[
            pl.BlockSpec((tm, k), lambda i, j: (i, 0)),
            pl.BlockSpec((HALO_ROWS, k), lambda i, j: (jnp.maximum(i * halo_blocks - 1, 0), 0)),
            pl.BlockSpec((1, k), lambda i, j: (0, 0)),
            pl.BlockSpec((k, tn), lambda i, j: (0, j)),
            pl.BlockSpec((k, tn), lambda i, j: (0, nj + j)),
            pl.BlockSpec((CONV_WIDTH, tn), lambda i, j: (0, j)),
            pl.BlockSpec((CONV_WIDTH, tn), lambda i, j: (0, nj + j)),
            pl.BlockSpec((1, tn), lambda i, j: (0, j)),
            pl.BlockSpec((1, tn), lambda i, j: (0, nj + j)),
        ],
        out_specs=pl.BlockSpec((tm, tn), lambda i, j: (i, j)),
        out_shape=jax.ShapeDtypeStruct((m, D_FF), BF16),
        scratch_shapes=[pltpu.VMEM((HALO_ROWS + tm, k), BF16)],
        compiler_params=_params("parallel", "arbitrary"),
        name="ffn_up",
    )(x, x, gain.reshape(1, k), w_up, w_up, conv_w, conv_w, cb, cb)


def _flash_kernel(qa_ref, qb_ref, ka_ref, kb_ref, v_ref, o_ref, m_ref, acc_ref,
                  s0, s1, p0, p1, a0, a1):
    i = pl.program_id(1)
    t = FLASH_TILE
    sbuf, pbuf, abuf = (s0, s1), (p0, p1), (a0, a1)

    def scores(j, slot):
        start = pl.multiple_of(j * t, t)
        keys = jnp.concatenate([ka_ref[pl.ds(start, t), :], kb_ref[pl.ds(start, t), :]], axis=1)
        q = jnp.concatenate([qa_ref[...], qb_ref[...]], axis=1)
        sbuf[slot][...] = lax.dot_general(q, keys, _NT, preferred_element_type=F32)

    def softmax(slot, causal=False):
        s = sbuf[slot][...]
        if causal:
            row = lax.broadcasted_iota(jnp.int32, (t, t), 0)
            col = lax.broadcasted_iota(jnp.int32, (t, t), 1)
            s = jnp.where(col <= row, s, NEG_INF)
        m_prev = m_ref[...]
        m_new = jnp.maximum(m_prev, jnp.max(s, axis=1, keepdims=True))
        abuf[slot][...] = jnp.exp2(m_prev - m_new)
        pbuf[slot][...] = jnp.exp2(s - jnp.tile(m_new, (1, t // LANES))).astype(BF16)
        m_ref[...] = m_new

    def accumulate(j, slot):
        start = pl.multiple_of(j * t, t)
        vals = jnp.concatenate([v_ref[pl.ds(start, t), :], jnp.ones((t, LANES), BF16)], axis=1)
        acc_ref[...] = (jnp.tile(abuf[slot][...], (1, 2)) * acc_ref[...]
                        + jnp.dot(pbuf[slot][...], vals, preferred_element_type=F32))

    m_ref[...] = jnp.full(m_ref.shape, MASKED, F32)
    acc_ref[...] = jnp.zeros(acc_ref.shape, F32)
    p1[...] = jnp.zeros(p1.shape, BF16)
    a1[...] = jnp.ones(a1.shape, F32)
    scores(0, 0)

    def pair(step, carry):
        j = 2 * step
        scores(j + 1, 1)
        accumulate(jnp.maximum(j - 1, 0), 1)
        softmax(0)
        scores(j + 2, 0)
        accumulate(j, 0)
        softmax(1)
        return carry

    pairs = i // 2
    lax.fori_loop(0, pairs, pair, 0)
    pending = jnp.maximum(2 * pairs - 1, 0)

    @pl.when(i % 2 == 0)
    def _():
        accumulate(pending, 1)
        softmax(0, causal=True)
        accumulate(i, 0)

    @pl.when(i % 2 == 1)
    def _():
        scores(i, 1)
        accumulate(pending, 1)
        softmax(0)
        accumulate(i - 1, 0)
        softmax(1, causal=True)
        accumulate(i, 1)

    acc = acc_ref[...]
    o_ref[...] = (acc[:, :LANES] / acc[:, LANES:]).astype(o_ref.dtype)


def _flash_attention(qa, qa_map, qb, qb_map, ka, ka_map, kb, kb_map, v, v_map, *, heads, name):
    s = ka.shape[0]
    t = FLASH_TILE
    assert s % t == 0
    return pl.pallas_call(
        _flash_kernel,
        grid=(heads, s // t),
        in_specs=[
            pl.BlockSpec((t, LANES), qa_map),
            pl.BlockSpec((t, LANES), qb_map),
            pl.BlockSpec((s, LANES), ka_map),
            pl.BlockSpec((s, LANES), kb_map),
            pl.BlockSpec((s, LANES), v_map),
        ],
        out_specs=pl.BlockSpec((t, LANES), lambda h, i: (i, h)),
        out_shape=jax.ShapeDtypeStruct((s, heads * LANES), BF16),
        scratch_shapes=[
            pltpu.VMEM((t, LANES), F32),
            pltpu.VMEM((t, 2 * LANES), F32),
            pltpu.VMEM((t, t), F32), pltpu.VMEM((t, t), F32),
            pltpu.VMEM((t, t), BF16), pltpu.VMEM((t, t), BF16),
            pltpu.VMEM((t, LANES), F32), pltpu.VMEM((t, LANES), F32),
        ],
        compiler_params=_params("parallel", "arbitrary"),
        name=name,
    )(qa, qb, ka, kb, v)


def _moba_select_kernel(q_ref, k_ref, bias_ref, kmh_ref, kml_ref, *, nb):
    i = pl.program_id(1)
    bs = MOBA_BLOCK
    tq = q_ref.shape[0]

    @pl.when(i == 0)
    def _():
        kmh_ref[...] = jnp.zeros(kmh_ref.shape, F32)
        kml_ref[...] = jnp.zeros(kml_ref.shape, F32)
        for b in range(nb):
            km = jnp.mean(k_ref[b * bs:(b + 1) * bs, :].astype(F32), axis=0, keepdims=True)
            hi = km.astype(BF16).astype(F32)
            kmh_ref[b:b + 1, :] = hi
            kml_ref[b:b + 1, :] = km - hi

    q = q_ref[...]
    gate = (lax.dot_general(q, kmh_ref[...].astype(BF16), _NT, preferred_element_type=F32)
            + lax.dot_general(q, kml_ref[...].astype(BF16), _NT, preferred_element_type=F32))
    blk = lax.broadcasted_iota(jnp.int32, gate.shape, 1).astype(F32)
    row = lax.broadcasted_iota(jnp.int32, gate.shape, 0)
    own = (i * (tq // bs) + jnp.right_shift(row, bs.bit_length() - 1)).astype(F32)
    gate = jnp.where(blk < own, gate, NEG_INF)
    sel = jnp.zeros(gate.shape, F32)
    for r in range(min(MOBA_TOPK, nb)):
        mx = jnp.max(gate, axis=1, keepdims=True)
        first = jnp.min(jnp.where(gate == mx, blk, float(LANES)), axis=1, keepdims=True)
        pick = blk == first
        sel = jnp.maximum(sel, jnp.where(pick, jnp.where(own > r, 1.0, 0.0), 0.0))
        gate = jnp.where(pick, NEG_INF, gate)
    visible = jnp.where(blk == own, 1.0, sel)
    bias = jnp.where(visible > 0.5, 0.0, MASKED)
    bias_ref[...] = jnp.where(blk < nb, bias, 0.0).astype(BF16)


def _moba_attention(qkv, *, tq=512):
    s = qkv.shape[0]
    bs = MOBA_BLOCK
    assert s % bs == 0 and s % tq == 0 and tq % bs == 0
    nb = s // bs
    assert nb <= LANES
    bias = pl.pallas_call(
        functools.partial(_moba_select_kernel, nb=nb),
        grid=(N_HEADS, s // tq),
        in_specs=[
            pl.BlockSpec((tq, HEAD_DIM), lambda h, i: (i, h)),
            pl.BlockSpec((s, HEAD_DIM), lambda h, i: (0, N_HEADS + h)),
        ],
        out_specs=pl.BlockSpec((tq, LANES), lambda h, i: (i, h)),
        out_shape=jax.ShapeDtypeStruct((s, N_HEADS * LANES), BF16),
        scratch_shapes=[pltpu.VMEM((LANES, HEAD_DIM), F32), pltpu.VMEM((LANES, HEAD_DIM), F32)],
        compiler_params=_params("parallel", "arbitrary"),
        name="moba_select",
    )(qkv, qkv)
    key_block = (jnp.arange(s, dtype=jnp.int32) // bs)[:, None]
    onehot = (key_block == jnp.arange(LANES, dtype=jnp.int32)[None, :]).astype(BF16)
    return _flash_attention(
        qkv, lambda h, i: (i, h), bias, lambda h, i: (i, h),
        qkv, lambda h, i: (0, N_HEADS + h), onehot, lambda h, i: (0, 0),
        qkv, lambda h, i: (0, 2 * N_HEADS + h), heads=N_HEADS, name="moba_flash")


def _mla_down_kernel(x_ref, g_ref, w_ref, qn_ref, kvn_ref, cos_ref, sin_ref,
                     cq_ref, ckv_ref, kr_ref):
    xn = _rms_normalize(x_ref[...], g_ref[...]).astype(BF16)
    lat = jnp.dot(xn, w_ref[...], preferred_element_type=F32)
    q0, kv0, r0, rr0 = 0, MLA_Q_RANK, MLA_Q_RANK + MLA_KV_RANK, MLA_Q_RANK + MLA_KV_RANK + LANES
    cq_ref[...] = _rms_normalize(lat[:, q0:kv0], qn_ref[...]).astype(BF16)
    ckv_ref[...] = _rms_normalize(lat[:, kv0:r0], kvn_ref[...]).astype(BF16)
    kr_ref[...] = (lat[:, r0:rr0] * cos_ref[...] + lat[:, rr0:rr0 + LANES] * sin_ref[...]).astype(BF16)


def _mla_down(x, gain, w_ext, q_norm, kv_norm, cos_r, sin_r, *, tm=512):
    m, k = x.shape
    n = w_ext.shape[1]
    assert m % tm == 0
    row = lambda i: (i, 0)
    fixed = lambda i: (0, 0)
    return pl.pallas_call(
        _mla_down_kernel,
        grid=(m // tm,),
        in_specs=[
            pl.BlockSpec((tm, k), row),
            pl.BlockSpec((1, k), fixed),
            pl.BlockSpec((k, n), fixed),
            pl.BlockSpec((1, MLA_Q_RANK), fixed),
            pl.BlockSpec((1, MLA_KV_RANK), fixed),
            pl.BlockSpec((tm, LANES), row),
            pl.BlockSpec((tm, LANES), row),
        ],
        out_specs=[
            pl.BlockSpec((tm, MLA_Q_RANK), row),
            pl.BlockSpec((tm, MLA_KV_RANK), row),
            pl.BlockSpec((tm, LANES), row),
        ],
        out_shape=[
            jax.ShapeDtypeStruct((m, MLA_Q_RANK), BF16),
            jax.ShapeDtypeStruct((m, MLA_KV_RANK), BF16),
            jax.ShapeDtypeStruct((m, LANES), BF16),
        ],
        compiler_params=_params("parallel"),
        name="mla_down",
    )(x, gain.reshape(1, k), w_ext, q_norm.reshape(1, -1), kv_norm.reshape(1, -1), cos_r, sin_r)


def _mla_up_kernel(cq_ref, ckv_ref, wq_ref, wkv_ref, cos_ref, sin_ref, q_ref, kv_ref, *, q_scale):
    cq = cq_ref[...]
    cos = cos_ref[...] * q_scale
    sin = sin_ref[...] * q_scale
    per_head = MLA_NOPE_DIM + 2 * LANES
    for h in range(MLA_HEADS):
        r = jnp.dot(cq, wq_ref[:, h * per_head:(h + 1) * per_head], preferred_element_type=F32)
        rope = r[:, MLA_NOPE_DIM:MLA_NOPE_DIM + LANES] * cos + r[:, MLA_NOPE_DIM + LANES:] * sin
        q_ref[:, 2 * h * LANES:(2 * h + 1) * LANES] = (r[:, :MLA_NOPE_DIM] * q_scale).astype(BF16)
        q_ref[:, (2 * h + 1) * LANES:(2 * h + 2) * LANES] = rope.astype(BF16)
    kv_ref[...] = jnp.dot(ckv_ref[...], wkv_ref[...], preferred_element_type=F32).astype(BF16)


def _mla_up(cq, ckv, wq_ext, wkv, cos_r, sin_r, *, tm=512):
    m = cq.shape[0]
    assert m % tm == 0 and MLA_NOPE_DIM == LANES
    row = lambda i: (i, 0)
    fixed = lambda i: (0, 0)
    nq = MLA_HEADS * 2 * LANES
    nkv = wkv.shape[1]
    kern = functools.partial(_mla_up_kernel,
                             q_scale=(MLA_NOPE_DIM + MLA_ROPE_DIM) ** -0.5 * LOG2E)
    return pl.pallas_call(
        kern,
        grid=(m // tm,),
        in_specs=[
            pl.BlockSpec((tm, MLA_Q_RANK), row),
            pl.BlockSpec((tm, MLA_KV_RANK), row),
            pl.BlockSpec(wq_ext.shape, fixed),
            pl.BlockSpec(wkv.shape, fixed),
            pl.BlockSpec((tm, LANES), row),
            pl.BlockSpec((tm, LANES), row),
        ],
        out_specs=[pl.BlockSpec((tm, nq), row), pl.BlockSpec((tm, nkv), row)],
        out_shape=[jax.ShapeDtypeStruct((m, nq), BF16), jax.ShapeDtypeStruct((m, nkv), BF16)],
        compiler_params=_params("parallel"),
        name="mla_up",
    )(cq, ckv, wq_ext, wkv, cos_r, sin_r)


def _mla_attention(q_full, kv, k_rope):
    return _flash_attention(
        q_full, lambda h, i: (i, 2 * h), q_full, lambda h, i: (i, 2 * h + 1),
        kv, lambda h, i: (0, h), k_rope, lambda h, i: (0, 0),
        kv, lambda h, i: (0, MLA_HEADS + h), heads=MLA_HEADS, name="mla_flash")


def _dilated_kernel(q_ref, k_ref, v_ref, o_ref, lse_ref):
    band = DIL_BAND
    nb = q_ref.shape[0] // band
    unroll = min(DIL_UNROLL, nb)
    row = lax.broadcasted_iota(jnp.int32, (band, 2 * band), 0)
    col = lax.broadcasted_iota(jnp.int32, (band, 2 * band), 1)
    outside = jnp.where(col < band, row - col, col - band - row)
    bias_inner = jnp.where(outside <= 0, 0.0, NEG_INF)
    bias_first = jnp.where(col <= row, 0.0, NEG_INF)

    def block(n, bias):
        qs = pl.multiple_of(n * band, band)
        ks = pl.multiple_of(jnp.maximum(n - 1, 0) * band, band)
        s = lax.dot_general(q_ref[pl.ds(qs, band), :], k_ref[pl.ds(ks, 2 * band), :], _NT,
                            preferred_element_type=F32) + bias
        m = jnp.max(s, axis=1, keepdims=True)
        p = jnp.exp2(s - m)
        l = jnp.sum(p, axis=1, keepdims=True)
        o = jnp.dot(p.astype(BF16), v_ref[pl.ds(ks, 2 * band), :], preferred_element_type=F32)
        o_ref[pl.ds(qs, band), :] = o / l
        lse_ref[pl.ds(qs, band), :] = jnp.broadcast_to(m + jnp.log2(l), (band, HEAD_DIM))

    def body(step, carry):
        n0 = step * unroll
        block(n0, jnp.where(n0 == 0, bias_first, bias_inner))
        for u in range(1, unroll):
            block(n0 + u, bias_inner)
        return carry

    lax.fori_loop(0, nb // unroll, body, 0)


def _dilated_pattern(qkv, dil):
    s = qkv.shape[0]
    rows = s // dil
    assert s % dil == 0 and rows % DIL_BAND == 0 and rows >= 2 * DIL_BAND
    nb = rows // DIL_BAND
    assert nb % min(DIL_UNROLL, nb) == 0
    view = qkv.reshape(rows, dil * qkv.shape[1])
    per_res = qkv.shape[1] // HEAD_DIM
    out_sds = jax.ShapeDtypeStruct((rows, dil * N_HEADS * HEAD_DIM), F32)
    out_spec = pl.BlockSpec((rows, HEAD_DIM), lambda h, r: (0, r * N_HEADS + h))
    o, lse = pl.pallas_call(
        _dilated_kernel,
        grid=(N_HEADS, dil),
        in_specs=[
            pl.BlockSpec((rows, HEAD_DIM), lambda h, r: (0, r * per_res + h)),
            pl.BlockSpec((rows, HEAD_DIM), lambda h, r: (0, r * per_res + N_HEADS + h)),
            pl.BlockSpec((rows, HEAD_DIM), lambda h, r: (0, r * per_res + 2 * N_HEADS + h)),
        ],
        out_specs=[out_spec, out_spec],
        out_shape=[out_sds, out_sds],
        compiler_params=_params("parallel", "parallel"),
        name=f"dilated_d{dil}",
    )(view, view, view)
    return o.reshape(s, N_HEADS * HEAD_DIM), lse.reshape(s, N_HEADS * HEAD_DIM)


def _dil_merge_kernel(o1, o2, o3, l1, l2, l3, out_ref):
    a, b, c = l1[...], l2[...], l3[...]
    mx = jnp.maximum(jnp.maximum(a, b), c)
    ea, eb, ec = jnp.exp2(a - mx), jnp.exp2(b - mx), jnp.exp2(c - mx)
    num = ea * o1[...] + eb * o2[...] + ec * o3[...]
    out_ref[...] = (num / (ea + eb + ec)).astype(out_ref.dtype)


def _dilated_attention(qkv, *, tm=512):
    outs, lses = zip(*[_dilated_pattern(qkv, dil) for _, dil in DIL_PATTERNS])
    s, n = outs[0].shape
    assert s % tm == 0
    spec = pl.BlockSpec((tm, n), lambda i: (i, 0))
    return pl.pallas_call(
        _dil_merge_kernel,
        grid=(s // tm,),
        in_specs=[spec] * 6,
        out_specs=spec,
        out_shape=jax.ShapeDtypeStruct((s, n), BF16),
        compiler_params=_params("parallel"),
        name="dilated_merge",
    )(*outs, *lses)


def _final_norm_kernel(x_ref, g_ref, o_ref):
    o_ref[...] = _rms_normalize(x_ref[...], g_ref[...])


def _final_norm(x, gain, *, tm=512):
    m, k = x.shape
    assert m % tm == 0
    return pl.pallas_call(
        _final_norm_kernel,
        grid=(m // tm,),
        in_specs=[pl.BlockSpec((tm, k), lambda i: (i, 0)), pl.BlockSpec((1, k), lambda i: (0, 0))],
        out_specs=pl.BlockSpec((tm, k), lambda i: (i, 0)),
        out_shape=jax.ShapeDtypeStruct((m, k), F32),
        compiler_params=_params("parallel"),
        name="final_norm",
    )(x, gain.reshape(1, k))


def _rope_tables(n_pos, dim):
    inv = 1.0 / (ROPE_THETA ** (jnp.arange(0, dim, 2, dtype=F32) / dim))
    ang = jnp.arange(n_pos, dtype=F32)[:, None] * inv[None, :]
    return jnp.cos(ang), jnp.sin(ang)


def _rotate_half_columns(w):
    half = w.shape[1] // 2
    return jnp.concatenate([-w[:, half:], w[:, :half]], axis=1)


def _pad_cols(w, width):
    return jnp.pad(w, ((0, 0), (0, width - w.shape[1])))


def _mla_weights(w_down, w_uq, w_ukv):
    lat = MLA_Q_RANK + MLA_KV_RANK
    w_rope = w_down[:, lat:]
    w_down_ext = jnp.concatenate(
        [w_down[:, :lat], _pad_cols(w_rope, LANES), _pad_cols(_rotate_half_columns(w_rope), LANES)],
        axis=1).astype(BF16)
    wq = w_uq.reshape(MLA_Q_RANK, MLA_HEADS, MLA_NOPE_DIM + MLA_ROPE_DIM)
    heads = []
    for h in range(MLA_HEADS):
        w_r = wq[:, h, MLA_NOPE_DIM:]
        heads += [wq[:, h, :MLA_NOPE_DIM], _pad_cols(w_r, LANES),
                  _pad_cols(_rotate_half_columns(w_r), LANES)]
    wq_ext = jnp.concatenate(heads, axis=1).astype(BF16)
    wkv = w_ukv.reshape(MLA_KV_RANK, MLA_HEADS, MLA_NOPE_DIM + MLA_V_DIM)
    wkv = jnp.concatenate([wkv[:, :, :MLA_NOPE_DIM].reshape(MLA_KV_RANK, -1),
                           wkv[:, :, MLA_NOPE_DIM:].reshape(MLA_KV_RANK, -1)], axis=1).astype(BF16)
    return w_down_ext, wq_ext, wkv


def kernel(x, attn_norm, ffn_norm, final_norm, moba_w_qkv, moba_w_o, mla_w_down, mla_q_norm,
           mla_kv_norm, mla_w_uq, mla_w_ukv, mla_w_o, dil_w_qkv, dil_w_o, ffn_w_up, ffn_conv_w,
           ffn_conv_b, ffn_w_down):
    b, s, d = x.shape
    assert b == 1 and d == D_MODEL
    cos, sin = _rope_tables(s, HEAD_DIM)
    cos_full = jnp.concatenate([cos, cos], axis=1)
    sin_signed = jnp.concatenate([-sin, sin], axis=1)
    cos_r, sin_r = _rope_tables(s, MLA_ROPE_DIM)
    cos_r = _pad_cols(jnp.concatenate([cos_r, cos_r], axis=1), LANES)
    sin_r = _pad_cols(jnp.concatenate([sin_r, sin_r], axis=1), LANES)

    h = x.reshape(s, d)
    for i in range(DEPTH):
        j = i // N_MIXERS
        kind = i % N_MIXERS
        if kind == 0:
            qkv = _qkv_projection(h, attn_norm[i], moba_w_qkv[j].astype(BF16), cos_full, sin_signed)
            mix = _moba_attention(qkv)
            w_o = moba_w_o[j]
        elif kind == 1:
            w_down_ext, wq_ext, wkv = _mla_weights(mla_w_down[j], mla_w_uq[j], mla_w_ukv[j])
            cq, ckv, k_rope = _mla_down(h, attn_norm[i], w_down_ext, mla_q_norm[j], mla_kv_norm[j],
                                        cos_r, sin_r)
            q_full, kv = _mla_up(cq, ckv, wq_ext, wkv, cos_r, sin_r)
            mix = _mla_attention(q_full, kv, k_rope)
            w_o = mla_w_o[j]
        else:
            qkv = _qkv_projection(h, attn_norm[i], dil_w_qkv[j].astype(BF16), cos_full, sin_signed)
            mix = _dilated_attention(qkv)
            w_o = dil_w_o[j]
        h = _matmul_residual(mix, w_o.astype(BF16), h, tn=1024)
        act = _ffn_up(h, ffn_norm[i], ffn_w_up[i].astype(BF16), ffn_conv_w[i], ffn_conv_b[i])
        h = _matmul_residual(act, ffn_w_down[i].astype(BF16), h, tn=512)
    return _final_norm(h, final_norm).reshape(b, s, d)
```

```python
import functools
import math

import jax
import jax.numpy as jnp
from jax import lax
from jax.experimental import pallas as pl
from jax.experimental.pallas import tpu as pltpu

F32 = jnp.float32
BF16 = jnp.bfloat16

D_MODEL = 2048
DEPTH = 4
N_MIXERS = 3
ROPE_THETA = 10000.0
NORM_EPS = 1e-6

N_HEADS = 16
HEAD_DIM = D_MODEL // N_HEADS
MOBA_BLOCK = 256
MOBA_TOPK = 3

MLA_HEADS = 16
MLA_Q_RANK = 512
MLA_KV_RANK = 512
MLA_NOPE_DIM = 128
MLA_ROPE_DIM = 64
MLA_V_DIM = 128

DIL_PATTERNS = ((128, 1), (512, 4), (2048, 16))
DIL_BAND = 128
DIL_UNROLL = 4

D_FF = 5632
CONV_WIDTH = 3

LANES = 128
HALO_ROWS = 16
VMEM_LIMIT_BYTES = 56 * 1024 * 1024
FLASH_TILE = 512

LOG2E = math.log2(math.e)
NEG_INF = float("-inf")
MASKED = -1e30

_NT = (((1,), (1,)), ((), ()))


def _params(*semantics):
    return pltpu.CompilerParams(dimension_semantics=semantics,
                                vmem_limit_bytes=VMEM_LIMIT_BYTES)


def _rms_normalize(x, gain):
    ms = jnp.mean(x * x, axis=-1, keepdims=True)
    return x * lax.rsqrt(ms + NORM_EPS) * gain


def _norm_matmul_kernel(x_ref, g_ref, w_ref, cos_ref, sin_ref, o_ref, xn_ref, *,
                        q_tiles, rope_tiles, q_scale):
    j = pl.program_id(1)

    @pl.when(j == 0)
    def _():
        xn_ref[...] = _rms_normalize(x_ref[...], g_ref[...]).astype(BF16)

    acc = jnp.dot(xn_ref[...], w_ref[...], preferred_element_type=F32)

    @pl.when(j < rope_tiles)
    def _():
        col_scale = jnp.where(j < q_tiles, q_scale, 1.0)
        cos = cos_ref[...] * col_scale
        sin = sin_ref[...] * col_scale
        for c in range(acc.shape[1] // HEAD_DIM):
            xh = acc[:, c * HEAD_DIM:(c + 1) * HEAD_DIM]
            roped = xh * cos + pltpu.roll(xh, HEAD_DIM // 2, axis=1) * sin
            o_ref[:, c * HEAD_DIM:(c + 1) * HEAD_DIM] = roped.astype(o_ref.dtype)

    @pl.when(j >= rope_tiles)
    def _():
        o_ref[...] = acc.astype(o_ref.dtype)


def _qkv_projection(x, gain, w, cos_full, sin_signed, *, tm=1024, tn=1024):
    m, k = x.shape
    n = w.shape[1]
    width = N_HEADS * HEAD_DIM
    assert m % tm == 0 and n % tn == 0 and width % tn == 0 and n == 3 * width
    kern = functools.partial(_norm_matmul_kernel, q_tiles=width // tn, rope_tiles=2 * width // tn,
                             q_scale=HEAD_DIM ** -0.5 * LOG2E)
    return pl.pallas_call(
        kern,
        grid=(m // tm, n // tn),
        in_specs=[
            pl.BlockSpec((tm, k), lambda i, j: (i, 0)),
            pl.BlockSpec((1, k), lambda i, j: (0, 0)),
            pl.BlockSpec((k, tn), lambda i, j: (0, j)),
            pl.BlockSpec((tm, HEAD_DIM), lambda i, j: (i, 0)),
            pl.BlockSpec((tm, HEAD_DIM), lambda i, j: (i, 0)),
        ],
        out_specs=pl.BlockSpec((tm, tn), lambda i, j: (i, j)),
        out_shape=jax.ShapeDtypeStruct((m, n), BF16),
        scratch_shapes=[pltpu.VMEM((tm, k), BF16)],
        compiler_params=_params("parallel", "arbitrary"),
        name="qkv_projection",
    )(x, gain.reshape(1, k), w, cos_full, sin_signed)


def _matmul_residual_kernel(a_ref, w_ref, r_ref, o_ref):
    o_ref[...] = r_ref[...] + jnp.dot(a_ref[...], w_ref[...], preferred_element_type=F32)


def _matmul_residual(a, w, res, *, tm=1024, tn=512):
    m, k = a.shape
    n = w.shape[1]
    assert m % tm == 0 and n % tn == 0
    return pl.pallas_call(
        _matmul_residual_kernel,
        grid=(m // tm, n // tn),
        in_specs=[
            pl.BlockSpec((tm, k), lambda i, j: (i, 0)),
            pl.BlockSpec((k, tn), lambda i, j: (0, j)),
            pl.BlockSpec((tm, tn), lambda i, j: (i, j)),
        ],
        out_specs=pl.BlockSpec((tm, tn), lambda i, j: (i, j)),
        out_shape=jax.ShapeDtypeStruct((m, n), F32),
        compiler_params=_params("parallel", "arbitrary"),
        name="matmul_residual",
    )(a, w, res)


def _ffn_up_kernel(x_ref, halo_ref, g_ref, wg_ref, wv_ref, cwg_ref, cwv_ref, cbg_ref, cbv_ref,
                   o_ref, xn_ref):
    i = pl.program_id(0)
    j = pl.program_id(1)

    @pl.when(j == 0)
    def _():
        gain = g_ref[...]
        halo = _rms_normalize(halo_ref[...], gain)
        halo = jnp.where(i == 0, 0.0, halo)
        xn_ref[0:HALO_ROWS, :] = halo.astype(BF16)
        xn_ref[HALO_ROWS:, :] = _rms_normalize(x_ref[...], gain).astype(BF16)

    def conv(w_ref, cw_ref, cb_ref):
        h = jnp.dot(xn_ref[...], w_ref[...], preferred_element_type=F32)
        cw = cw_ref[...]
        y = (pltpu.roll(h, 2, axis=0) * cw[0:1, :] + pltpu.roll(h, 1, axis=0) * cw[1:2, :]
             + h * cw[2:3, :])
        return y[HALO_ROWS:, :] + cb_ref[...]

    gate = conv(wg_ref, cwg_ref, cbg_ref)
    val = conv(wv_ref, cwv_ref, cbv_ref)
    o_ref[...] = (gate * (1.0 / (1.0 + jnp.exp(-gate))) * val).astype(o_ref.dtype)


def _ffn_up(x, gain, w_up, conv_w, conv_b, *, tm=1024, tn=512):
    m, k = x.shape
    assert m % tm == 0 and D_FF % tn == 0 and tm % HALO_ROWS == 0
    nj = D_FF // tn
    halo_blocks = tm // HALO_ROWS
    cb = conv_b.reshape(1, 2 * D_FF)
    return pl.pallas_call(
        _ffn_up_kernel,
        grid=(m // tm, nj),
        in_specs=[
            pl.BlockSpec((tm, k), lambda i, j: (i, 0)),
            pl.BlockSpec((HALO_ROWS, k), lambda i, j: (jnp.maximum(i * halo_blocks - 1, 0), 0)),
            pl.BlockSpec((1, k), lambda i, j: (0, 0)),
            pl.BlockSpec((k, tn), lambda i, j: (0, j)),
            pl.BlockSpec((k, tn), lambda i, j: (0, nj + j)),
            pl.BlockSpec((CONV_WIDTH, tn), lambda i, j: (0, j)),
            pl.BlockSpec((CONV_WIDTH, tn), lambda i, j: (0, nj + j)),
            pl.BlockSpec((1, tn), lambda i, j: (0, j)),
            pl.BlockSpec((1, tn), lambda i, j: (0, nj + j)),
        ],
        out_specs=pl.BlockSpec((tm, tn), lambda i, j: (i, j)),
        out_shape=jax.ShapeDtypeStruct((m, D_FF), BF16),
        scratch_shapes=[pltpu.VMEM((HALO_ROWS + tm, k), BF16)],
        compiler_params=_params("parallel", "arbitrary"),
        name="ffn_up",
    )(x, x, gain.reshape(1, k), w_up, w_up, conv_w, conv_w, cb, cb)


def _flash_kernel(qa_ref, qb_ref, ka_ref, kb_ref, v_ref, o_ref, m_ref, acc_ref,
                  s0, s1, p0, p1, a0, a1):
    i = pl.program_id(1)
    t = FLASH_TILE
    sbuf, pbuf, abuf = (s0, s1), (p0, p1), (a0, a1)

    def scores(j, slot):
        start = pl.multiple_of(j * t, t)
        keys = jnp.concatenate([ka_ref[pl.ds(start, t), :], kb_ref[pl.ds(start, t), :]], axis=1)
        q = jnp.concatenate([qa_ref[...], qb_ref[...]], axis=1)
        sbuf[slot][...] = lax.dot_general(q, keys, _NT, preferred_element_type=F32)

    def softmax(slot, causal=False):
        s = sbuf[slot][...]
        if causal:
            row = lax.broadcasted_iota(jnp.int32, (t, t), 0)
            col = lax.broadcasted_iota(jnp.int32, (t, t), 1)
            s = jnp.where(col <= row, s, NEG_INF)
        m_prev = m_ref[...]
        m_new = jnp.maximum(m_prev, jnp.max(s, axis=1, keepdims=True))
        abuf[slot][...] = jnp.exp2(m_prev - m_new)
        pbuf[slot][...] = jnp.exp2(s - jnp.tile(m_new, (1, t // LANES))).astype(BF16)
        m_ref[...] = m_new

    def accumulate(j, slot):
        start = pl.multiple_of(j * t, t)
        vals = jnp.concatenate([v_ref[pl.ds(start, t), :], jnp.ones((t, LANES), BF16)], axis=1)
        acc_ref[...] = (jnp.tile(abuf[slot][...], (1, 2)) * acc_ref[...]
                        + jnp.dot(pbuf[slot][...], vals, preferred_element_type=F32))

    m_ref[...] = jnp.full(m_ref.shape, MASKED, F32)
    acc_ref[...] = jnp.zeros(acc_ref.shape, F32)
    scores(i, 0)
    scores(0, 1)
    softmax(0, causal=True)

    def pair(k, carry):
        scores(2 * k + 1, 0)
        accumulate(jnp.where(k == 0, i, 2 * k - 1), 0)
        softmax(1)
        scores(jnp.minimum(2 * k + 2, i - 1), 1)
        accumulate(2 * k, 1)
        softmax(0)
        return carry

    lax.fori_loop(0, i // 2, pair, 0)

    @pl.when(i % 2 == 0)
    def _():
        accumulate(jnp.where(i == 0, 0, i - 1), 0)

    @pl.when(i % 2 == 1)
    def _():
        accumulate(jnp.where(i == 1, i, i - 2), 0)
        softmax(1)
        accumulate(i - 1, 1)

    acc = acc_ref[...]
    o_ref[...] = (acc[:, :LANES] / acc[:, LANES:]).astype(o_ref.dtype)


def _flash_attention(qa, qa_map, qb, qb_map, ka, ka_map, kb, kb_map, v, v_map, *, heads, name):
    s = ka.shape[0]
    t = FLASH_TILE
    assert s % t == 0
    return pl.pallas_call(
        _flash_kernel,
        grid=(heads, s // t),
        in_specs=[
            pl.BlockSpec((t, LANES), qa_map),
            pl.BlockSpec((t, LANES), qb_map),
            pl.BlockSpec((s, LANES), ka_map),
            pl.BlockSpec((s, LANES), kb_map),
            pl.BlockSpec((s, LANES), v_map),
        ],
        out_specs=pl.BlockSpec((t, LANES), lambda h, i: (i, h)),
        out_shape=jax.ShapeDtypeStruct((s, heads * LANES), BF16),
        scratch_shapes=[
            pltpu.VMEM((t, LANES), F32),
            pltpu.VMEM((t, 2 * LANES), F32),
            pltpu.VMEM((t, t), F32), pltpu.VMEM((t, t), F32),
            pltpu.VMEM((t, t), BF16), pltpu.VMEM((t, t), BF16),
            pltpu.VMEM((t, LANES), F32), pltpu.VMEM((t, LANES), F32),
        ],
        compiler_params=_params("parallel", "arbitrary"),
        name=name,
    )(qa, qb, ka, kb, v)


def _moba_select_kernel(q_ref, k_ref, bias_ref, kmh_ref, kml_ref, *, nb):
    i = pl.program_id(1)
    bs = MOBA_BLOCK
    tq = q_ref.shape[0]

    @pl.when(i == 0)
    def _():
        kmh_ref[...] = jnp.zeros(kmh_ref.shape, F32)
        kml_ref[...] = jnp.zeros(kml_ref.shape, F32)
        for b in range(nb):
            km = jnp.mean(k_ref[b * bs:(b + 1) * bs, :].astype(F32), axis=0, keepdims=True)
            hi = km.astype(BF16).astype(F32)
            kmh_ref[b:b + 1, :] = hi
            kml_ref[b:b + 1, :] = km - hi

    nbp = kmh_ref.shape[0]
    q = q_ref[...]
    gate = (lax.dot_general(kmh_ref[...].astype(BF16), q, _NT, preferred_element_type=F32)
            + lax.dot_general(kml_ref[...].astype(BF16), q, _NT, preferred_element_type=F32))
    blk = lax.broadcasted_iota(jnp.int32, gate.shape, 0).astype(F32)
    pos = lax.broadcasted_iota(jnp.int32, gate.shape, 1)
    own = (i * (tq // bs) + jnp.right_shift(pos, bs.bit_length() - 1)).astype(F32)
    gate = jnp.where(blk < own, gate, NEG_INF)
    sel = jnp.zeros(gate.shape, F32)
    for r in range(min(MOBA_TOPK, nb)):
        mx = jnp.max(gate, axis=0, keepdims=True)
        first = jnp.min(jnp.where(gate == mx, blk, float(nbp)), axis=0, keepdims=True)
        pick = blk == first
        sel = jnp.maximum(sel, jnp.where(pick, jnp.where(own > r, 1.0, 0.0), 0.0))
        gate = jnp.where(pick, NEG_INF, gate)
    visible = jnp.where(blk == own, 1.0, sel)
    bias = jnp.where(visible > 0.5, 0.0, MASKED)
    bias = jnp.where(blk < nb, bias, 0.0)
    padded = jnp.concatenate([bias, jnp.zeros((LANES - nbp, tq), F32)], axis=0)
    bias_ref[...] = padded.T.astype(BF16)


def _moba_attention(qkv, *, tq=1024):
    s = qkv.shape[0]
    bs = MOBA_BLOCK
    assert s % bs == 0 and s % tq == 0 and tq % bs == 0
    nb = s // bs
    nbp = -(-nb // 8) * 8
    assert nbp < LANES
    bias = pl.pallas_call(
        functools.partial(_moba_select_kernel, nb=nb),
        grid=(N_HEADS, s // tq),
        in_specs=[
            pl.BlockSpec((tq, HEAD_DIM), lambda h, i: (i, h)),
            pl.BlockSpec((s, HEAD_DIM), lambda h, i: (0, N_HEADS + h)),
        ],
        out_specs=pl.BlockSpec((tq, LANES), lambda h, i: (i, h)),
        out_shape=jax.ShapeDtypeStruct((s, N_HEADS * LANES), BF16),
        scratch_shapes=[pltpu.VMEM((nbp, HEAD_DIM), F32), pltpu.VMEM((nbp, HEAD_DIM), F32)],
        compiler_params=_params("parallel", "arbitrary"),
        name="moba_select",
    )(qkv, qkv)
    key_block = (jnp.arange(s, dtype=jnp.int32) // bs)[:, None]
    onehot = (key_block == jnp.arange(LANES, dtype=jnp.int32)[None, :]).astype(BF16)
    return _flash_attention(
        qkv, lambda h, i: (i, h), bias, lambda h, i: (i, h),
        qkv, lambda h, i: (0, N_HEADS + h), onehot, lambda h, i: (0, 0),
        qkv, lambda h, i: (0, 2 * N_HEADS + h), heads=N_HEADS, name="moba_flash")


def _mla_down_kernel(x_ref, g_ref, w_ref, qn_ref, kvn_ref, cos_ref, sin_ref,
                     cq_ref, ckv_ref, kr_ref):
    xn = _rms_normalize(x_ref[...], g_ref[...]).astype(BF16)
    lat = jnp.dot(xn, w_ref[...], preferred_element_type=F32)
    q0, kv0, r0, rr0 = 0, MLA_Q_RANK, MLA_Q_RANK + MLA_KV_RANK, MLA_Q_RANK + MLA_KV_RANK + LANES
    cq_ref[...] = _rms_normalize(lat[:, q0:kv0], qn_ref[...]).astype(BF16)
    ckv_ref[...] = _rms_normalize(lat[:, kv0:r0], kvn_ref[...]).astype(BF16)
    kr_ref[...] = (lat[:, r0:rr0] * cos_ref[...] + lat[:, rr0:rr0 + LANES] * sin_ref[...]).astype(BF16)


def _mla_down(x, gain, w_ext, q_norm, kv_norm, cos_r, sin_r, *, tm=512):
    m, k = x.shape
    n = w_ext.shape[1]
    assert m % tm == 0
    row = lambda i: (i, 0)
    fixed = lambda i: (0, 0)
    return pl.pallas_call(
        _mla_down_kernel,
        grid=(m // tm,),
        in_specs=[
            pl.BlockSpec((tm, k), row),
            pl.BlockSpec((1, k), fixed),
            pl.BlockSpec((k, n), fixed),
            pl.BlockSpec((1, MLA_Q_RANK), fixed),
            pl.BlockSpec((1, MLA_KV_RANK), fixed),
            pl.BlockSpec((tm, LANES), row),
            pl.BlockSpec((tm, LANES), row),
        ],
        out_specs=[
            pl.BlockSpec((tm, MLA_Q_RANK), row),
            pl.BlockSpec((tm, MLA_KV_RANK), row),
            pl.BlockSpec((tm, LANES), row),
        ],
        out_shape=[
            jax.ShapeDtypeStruct((m, MLA_Q_RANK), BF16),
            jax.ShapeDtypeStruct((m, MLA_KV_RANK), BF16),
            jax.ShapeDtypeStruct((m, LANES), BF16),
        ],
        compiler_params=_params("parallel"),
        name="mla_down",
    )(x, gain.reshape(1, k), w_ext, q_norm.reshape(1, -1), kv_norm.reshape(1, -1), cos_r, sin_r)


def _mla_up_kernel(cq_ref, ckv_ref, wq_ref, wkv_ref, cos_ref, sin_ref, q_ref, kv_ref, *, q_scale):
    cq = cq_ref[...]
    cos = cos_ref[...] * q_scale
    sin = sin_ref[...] * q_scale
    per_head = MLA_NOPE_DIM + 2 * LANES
    for h in range(MLA_HEADS):
        r = jnp.dot(cq, wq_ref[:, h * per_head:(h + 1) * per_head], preferred_element_type=F32)
        rope = r[:, MLA_NOPE_DIM:MLA_NOPE_DIM + LANES] * cos + r[:, MLA_NOPE_DIM + LANES:] * sin
        q_ref[:, 2 * h * LANES:(2 * h + 1) * LANES] = (r[:, :MLA_NOPE_DIM] * q_scale).astype(BF16)
        q_ref[:, (2 * h + 1) * LANES:(2 * h + 2) * LANES] = rope.astype(BF16)
    kv_ref[...] = jnp.dot(ckv_ref[...], wkv_ref[...], preferred_element_type=F32).astype(BF16)


def _mla_up(cq, ckv, wq_ext, wkv, cos_r, sin_r, *, tm=512):
    m = cq.shape[0]
    assert m % tm == 0 and MLA_NOPE_DIM == LANES
    row = lambda i: (i, 0)
    fixed = lambda i: (0, 0)
    nq = MLA_HEADS * 2 * LANES
    nkv = wkv.shape[1]
    kern = functools.partial(_mla_up_kernel,
                             q_scale=(MLA_NOPE_DIM + MLA_ROPE_DIM) ** -0.5 * LOG2E)
    return pl.pallas_call(
        kern,
        grid=(m // tm,),
        in_specs=[
            pl.BlockSpec((tm, MLA_Q_RANK), row),
            pl.BlockSpec((tm, MLA_KV_RANK), row),
            pl.BlockSpec(wq_ext.shape, fixed),
            pl.BlockSpec(wkv.shape, fixed),
            pl.BlockSpec((tm, LANES), row),
            pl.BlockSpec((tm, LANES), row),
        ],
        out_specs=[pl.BlockSpec((tm, nq), row), pl.BlockSpec((tm, nkv), row)],
        out_shape=[jax.ShapeDtypeStruct((m, nq), BF16), jax.ShapeDtypeStruct((m, nkv), BF16)],
        compiler_params=_params("parallel"),
        name="mla_up",
    )(cq, ckv, wq_ext, wkv, cos_r, sin_r)


def _mla_attention(q_full, kv, k_rope):
    return _flash_attention(
        q_full, lambda h, i: (i, 2 * h), q_full, lambda h, i: (i, 2 * h + 1),
        kv, lambda h, i: (0, h), k_rope, lambda h, i: (0, 0),
        kv, lambda h, i: (0, MLA_HEADS + h), heads=MLA_HEADS, name="mla_flash")


def _dilated_kernel(q_ref, k_ref, v_ref, o_ref, qf, kf, vf, num, den, mx):
    band = DIL_BAND
    s_len = q_ref.shape[0]
    qf[...] = q_ref[...].astype(F32)
    kf[...] = k_ref[...].astype(F32)
    vf[...] = v_ref[...].astype(F32)

    row = lax.broadcasted_iota(jnp.int32, (band, 2 * band), 0)
    col = lax.broadcasted_iota(jnp.int32, (band, 2 * band), 1)
    outside = jnp.where(col < band, row - col, col - band - row)
    bias_inner = jnp.where(outside <= 0, 0.0, NEG_INF)
    bias_first = jnp.where(col <= row, 0.0, NEG_INF)
    ones = jnp.ones((2 * band, LANES), BF16)

    def rows(start, n, dil):
        return pl.ds(start, n) if dil == 1 else pl.ds(start, n, stride=dil)

    def block(dil, r, n, bias, first_pattern):
        q_start = n * band * dil + r
        k_start = jnp.maximum(n - 1, 0) * band * dil + r
        q = qf[rows(q_start, band, dil), :].astype(BF16)
        k = kf[rows(k_start, 2 * band, dil), :].astype(BF16)
        v = vf[rows(k_start, 2 * band, dil), :].astype(BF16)
        s = lax.dot_general(q, k, _NT, preferred_element_type=F32) + bias
        m = jnp.max(s, axis=1, keepdims=True)
        p = jnp.exp2(s - m).astype(BF16)
        pv = jnp.dot(p, jnp.concatenate([v, ones], axis=1), preferred_element_type=F32)
        m_blk = jnp.broadcast_to(m, (band, LANES))
        dst = rows(q_start, band, dil)
        if first_pattern:
            num[dst, :] = pv[:, :LANES]
            den[dst, :] = pv[:, LANES:]
            mx[dst, :] = m_blk
        else:
            m_old = mx[dst, :]
            m_new = jnp.maximum(m_old, m_blk)
            a_old = jnp.exp2(m_old - m_new)
            a_blk = jnp.exp2(m_blk - m_new)
            num[dst, :] = a_old * num[dst, :] + a_blk * pv[:, :LANES]
            den[dst, :] = a_old * den[dst, :] + a_blk * pv[:, LANES:]
            mx[dst, :] = m_new

    for idx, (_, dil) in enumerate(DIL_PATTERNS):
        nb = s_len // dil // band
        unroll = min(DIL_UNROLL, nb)

        def residue(r, carry, dil=dil, unroll=unroll, steps=nb // unroll, first=(idx == 0)):
            def step(t, c):
                n0 = t * unroll
                block(dil, r, n0, jnp.where(n0 == 0, bias_first, bias_inner), first)
                for u in range(1, unroll):
                    block(dil, r, n0 + u, bias_inner, first)
                return c

            lax.fori_loop(0, steps, step, 0)
            return carry

        lax.fori_loop(0, dil, residue, 0)

    o_ref[...] = (num[...] / den[...]).astype(o_ref.dtype)


def _dilated_attention(qkv):
    s = qkv.shape[0]
    for window, dil in DIL_PATTERNS:
        assert window // dil == DIL_BAND
        nb = s // dil // DIL_BAND
        assert s % (dil * DIL_BAND) == 0 and nb >= 2 and nb % min(DIL_UNROLL, nb) == 0
    return pl.pallas_call(
        _dilated_kernel,
        grid=(N_HEADS,),
        in_specs=[
            pl.BlockSpec((s, HEAD_DIM), lambda h: (0, h)),
            pl.BlockSpec((s, HEAD_DIM), lambda h: (0, N_HEADS + h)),
            pl.BlockSpec((s, HEAD_DIM), lambda h: (0, 2 * N_HEADS + h)),
        ],
        out_specs=pl.BlockSpec((s, HEAD_DIM), lambda h: (0, h)),
        out_shape=jax.ShapeDtypeStruct((s, N_HEADS * HEAD_DIM), BF16),
        scratch_shapes=[pltpu.VMEM((s, HEAD_DIM), F32)] * 3 + [pltpu.VMEM((s, LANES), F32)] * 3,
        compiler_params=_params("parallel"),
        name="dilated_attention",
    )(qkv, qkv, qkv)


def _final_norm_kernel(x_ref, g_ref, o_ref):
    o_ref[...] = _rms_normalize(x_ref[...], g_ref[...])


def _final_norm(x, gain, *, tm=512):
    m, k = x.shape
    assert m % tm == 0
    return pl.pallas_call(
        _final_norm_kernel,
        grid=(m // tm,),
        in_specs=[pl.BlockSpec((tm, k), lambda i: (i, 0)), pl.BlockSpec((1, k), lambda i: (0, 0))],
        out_specs=pl.BlockSpec((tm, k), lambda i: (i, 0)),
        out_shape=jax.ShapeDtypeStruct((m, k), F32),
        compiler_params=_params("parallel"),
        name="final_norm",
    )(x, gain.reshape(1, k))


def _rope_tables(n_pos, dim):
    inv = 1.0 / (ROPE_THETA ** (jnp.arange(0, dim, 2, dtype=F32) / dim))
    ang = jnp.arange(n_pos, dtype=F32)[:, None] * inv[None, :]
    return jnp.cos(ang), jnp.sin(ang)


def _rotate_half_columns(w):
    half = w.shape[1] // 2
    return jnp.concatenate([-w[:, half:], w[:, :half]], axis=1)


def _pad_cols(w, width):
    return jnp.pad(w, ((0, 0), (0, width - w.shape[1])))


def _mla_weights(w_down, w_uq, w_ukv):
    lat = MLA_Q_RANK + MLA_KV_RANK
    w_rope = w_down[:, lat:]
    w_down_ext = jnp.concatenate(
        [w_down[:, :lat], _pad_cols(w_rope, LANES), _pad_cols(_rotate_half_columns(w_rope), LANES)],
        axis=1).astype(BF16)
    wq = w_uq.reshape(MLA_Q_RANK, MLA_HEADS, MLA_NOPE_DIM + MLA_ROPE_DIM)
    heads = []
    for h in range(MLA_HEADS):
        w_r = wq[:, h, MLA_NOPE_DIM:]
        heads += [wq[:, h, :MLA_NOPE_DIM], _pad_cols(w_r, LANES),
                  _pad_cols(_rotate_half_columns(w_r), LANES)]
    wq_ext = jnp.concatenate(heads, axis=1).astype(BF16)
    wkv = w_ukv.reshape(MLA_KV_RANK, MLA_HEADS, MLA_NOPE_DIM + MLA_V_DIM)
    wkv = jnp.concatenate([wkv[:, :, :MLA_NOPE_DIM].reshape(MLA_KV_RANK, -1),
                           wkv[:, :, MLA_NOPE_DIM:].reshape(MLA_KV_RANK, -1)], axis=1).astype(BF16)
    return w_down_ext, wq_ext, wkv


def kernel(x, attn_norm, ffn_norm, final_norm, moba_w_qkv, moba_w_o, mla_w_down, mla_q_norm,
           mla_kv_norm, mla_w_uq, mla_w_ukv, mla_w_o, dil_w_qkv, dil_w_o, ffn_w_up, ffn_conv_w,
           ffn_conv_b, ffn_w_down):
    b, s, d = x.shape
    assert b == 1 and d == D_MODEL
    cos, sin = _rope_tables(s, HEAD_DIM)
    cos_full = jnp.concatenate([cos, cos], axis=1)
    sin_signed = jnp.concatenate([-sin, sin], axis=1)
    cos_r, sin_r = _rope_tables(s, MLA_ROPE_DIM)
    cos_r = _pad_cols(jnp.concatenate([cos_r, cos_r], axis=1), LANES)
    sin_r = _pad_cols(jnp.concatenate([sin_r, sin_r], axis=1), LANES)

    h = x.reshape(s, d)
    for i in range(DEPTH):
        j = i // N_MIXERS
        kind = i % N_MIXERS
        if kind == 0:
            qkv = _qkv_projection(h, attn_norm[i], moba_w_qkv[j].astype(BF16), cos_full, sin_signed)
            mix = _moba_attention(qkv)
            w_o = moba_w_o[j]
        elif kind == 1:
            w_down_ext, wq_ext, wkv = _mla_weights(mla_w_down[j], mla_w_uq[j], mla_w_ukv[j])
            cq, ckv, k_rope = _mla_down(h, attn_norm[i], w_down_ext, mla_q_norm[j], mla_kv_norm[j],
                                        cos_r, sin_r)
            q_full, kv = _mla_up(cq, ckv, wq_ext, wkv, cos_r, sin_r)
            mix = _mla_attention(q_full, kv, k_rope)
            w_o = mla_w_o[j]
        else:
            qkv = _qkv_projection(h, attn_norm[i], dil_w_qkv[j].astype(BF16), cos_full, sin_signed)
            mix = _dilated_attention(qkv)
            w_o = dil_w_o[j]
        h = _matmul_residual(mix, w_o.astype(BF16), h, tn=1024)
        act = _ffn_up(h, ffn_norm[i], ffn_w_up[i].astype(BF16), ffn_conv_w[i], ffn_conv_b[i])
        h = _matmul_residual(act, ffn_w_down[i].astype(BF16), h, tn=512)
    return _final_norm(h, final_norm).reshape(b, s, d)
```

```python
import functools
import math

import jax
import jax.numpy as jnp
from jax import lax
from jax.experimental import pallas as pl
from jax.experimental.pallas import tpu as pltpu

F32 = jnp.float32
BF16 = jnp.bfloat16

D_MODEL = 2048
DEPTH = 4
N_MIXERS = 3
ROPE_THETA = 10000.0
NORM_EPS = 1e-6

N_HEADS = 16
HEAD_DIM = D_MODEL // N_HEADS
MOBA_BLOCK = 256
MOBA_TOPK = 3

MLA_HEADS = 16
MLA_Q_RANK = 512
MLA_KV_RANK = 512
MLA_NOPE_DIM = 128
MLA_ROPE_DIM = 64
MLA_V_DIM = 128

DIL_PATTERNS = ((128, 1), (512, 4), (2048, 16))
DIL_BAND = 128
DIL_UNROLL = 8

D_FF = 5632
CONV_WIDTH = 3

LANES = 128
HALO_ROWS = 16
VMEM_LIMIT_BYTES = 56 * 1024 * 1024
FLASH_TILE = 512

LOG2E = math.log2(math.e)
NEG_INF = float("-inf")
MASKED = -1e30

_NT = (((1,), (1,)), ((), ()))


def _params(*semantics):
    return pltpu.CompilerParams(dimension_semantics=semantics,
                                vmem_limit_bytes=VMEM_LIMIT_BYTES)


def _rms_normalize(x, gain):
    ms = jnp.mean(x * x, axis=-1, keepdims=True)
    return x * lax.rsqrt(ms + NORM_EPS) * gain


def _norm_matmul_kernel(x_ref, g_ref, w_ref, cos_ref, sin_ref, o_ref, xn_ref, *,
                        q_tiles, rope_tiles, q_scale):
    j = pl.program_id(1)

    @pl.when(j == 0)
    def _():
        xn_ref[...] = _rms_normalize(x_ref[...], g_ref[...]).astype(BF16)

    acc = jnp.dot(xn_ref[...], w_ref[...], preferred_element_type=F32)

    @pl.when(j < rope_tiles)
    def _():
        col_scale = jnp.where(j < q_tiles, q_scale, 1.0)
        cos = cos_ref[...] * col_scale
        sin = sin_ref[...] * col_scale
        for c in range(acc.shape[1] // HEAD_DIM):
            xh = acc[:, c * HEAD_DIM:(c + 1) * HEAD_DIM]
            roped = xh * cos + pltpu.roll(xh, HEAD_DIM // 2, axis=1) * sin
            o_ref[:, c * HEAD_DIM:(c + 1) * HEAD_DIM] = roped.astype(o_ref.dtype)

    @pl.when(j >= rope_tiles)
    def _():
        o_ref[...] = acc.astype(o_ref.dtype)


def _qkv_projection(x, gain, w, layer, cos_full, sin_signed, *, tm=1024, tn=1024):
    m, k = x.shape
    n = w.shape[2]
    width = N_HEADS * HEAD_DIM
    assert m % tm == 0 and n % tn == 0 and width % tn == 0 and n == 3 * width
    kern = functools.partial(_norm_matmul_kernel, q_tiles=width // tn, rope_tiles=2 * width // tn,
                             q_scale=HEAD_DIM ** -0.5 * LOG2E)
    return pl.pallas_call(
        kern,
        grid=(m // tm, n // tn),
        in_specs=[
            pl.BlockSpec((tm, k), lambda i, j: (i, 0)),
            pl.BlockSpec((1, k), lambda i, j: (0, 0)),
            pl.BlockSpec((None, k, tn), lambda i, j: (layer, 0, j)),
            pl.BlockSpec((tm, HEAD_DIM), lambda i, j: (i, 0)),
            pl.BlockSpec((tm, HEAD_DIM), lambda i, j: (i, 0)),
        ],
        out_specs=pl.BlockSpec((tm, tn), lambda i, j: (i, j)),
        out_shape=jax.ShapeDtypeStruct((m, n), BF16),
        scratch_shapes=[pltpu.VMEM((tm, k), BF16)],
        compiler_params=_params("parallel", "arbitrary"),
        name="qkv_projection",
    )(x, gain.reshape(1, k), w, cos_full, sin_signed)


def _matmul_residual_kernel(a_ref, w_ref, r_ref, o_ref):
    o_ref[...] = r_ref[...] + jnp.dot(a_ref[...], w_ref[...], preferred_element_type=F32)


def _matmul_residual(a, w, layer, res, *, tm=1024, tn=512):
    m, k = a.shape
    n = w.shape[2]
    assert m % tm == 0 and n % tn == 0
    return pl.pallas_call(
        _matmul_residual_kernel,
        grid=(m // tm, n // tn),
        in_specs=[
            pl.BlockSpec((tm, k), lambda i, j: (i, 0)),
            pl.BlockSpec((None, k, tn), lambda i, j: (layer, 0, j)),
            pl.BlockSpec((tm, tn), lambda i, j: (i, j)),
        ],
        out_specs=pl.BlockSpec((tm, tn), lambda i, j: (i, j)),
        out_shape=jax.ShapeDtypeStruct((m, n), F32),
        compiler_params=_params("parallel", "arbitrary"),
        name="matmul_residual",
    )(a, w, res)


def _ffn_up_kernel(x_ref, halo_ref, g_ref, wg_ref, wv_ref, cwg_ref, cwv_ref, cbg_ref, cbv_ref,
                   o_ref, xn_ref):
    i = pl.program_id(0)
    j = pl.program_id(1)

    @pl.when(j == 0)
    def _():
        gain = g_ref[...]
        halo = _rms_normalize(halo_ref[...], gain)
        halo = jnp.where(i == 0, 0.0, halo)
        xn_ref[0:HALO_ROWS, :] = halo.astype(BF16)
        xn_ref[HALO_ROWS:, :] = _rms_normalize(x_ref[...], gain).astype(BF16)

    def conv(w_ref, cw_ref, cb_ref):
        h = jnp.dot(xn_ref[...], w_ref[...].astype(BF16), preferred_element_type=F32)
        cw = cw_ref[...]
        y = (pltpu.roll(h, 2, axis=0) * cw[0:1, :] + pltpu.roll(h, 1, axis=0) * cw[1:2, :]
             + h * cw[2:3, :])
        return y[HALO_ROWS:, :] + cb_ref[...]

    gate = conv(wg_ref, cwg_ref, cbg_ref)
    val = conv(wv_ref, cwv_ref, cbv_ref)
    o_ref[...] = (gate * (1.0 / (1.0 + jnp.exp(-gate))) * val).astype(o_ref.dtype)


def _ffn_up(x, gain, w_up, conv_w, conv_b, layer, *, tm=1024, tn=512):
    m, k = x.shape
    assert m % tm == 0 and D_FF % tn == 0 and tm % HALO_ROWS == 0
    nj = D_FF // tn
    halo_blocks = tm // HALO_ROWS
    cb = conv_b.reshape(conv_b.shape[0], 1, 2 * D_FF)
    return pl.pallas_call(
        _ffn_up_kernel,
        grid=(m // tm, nj),
        in_specs=[
            pl.BlockSpec((tm, k), lambda i, j: (i, 0)),
            pl.BlockSpec((HALO_ROWS, k), lambda i, j: (jnp.maximum(i * halo_blocks - 1, 0), 0)),
            pl.BlockSpec((1, k), lambda i, j: (0, 0)),
            pl.BlockSpec((None, k, tn), lambda i, j: (layer, 0, j)),
            pl.BlockSpec((None, k, tn), lambda i, j: (layer, 0, nj + j)),
            pl.BlockSpec((None, CONV_WIDTH, tn), lambda i, j: (layer, 0, j)),
            pl.BlockSpec((None, CONV_WIDTH, tn), lambda i, j: (layer, 0, nj + j)),
            pl.BlockSpec((None, 1, tn), lambda i, j: (layer, 0, j)),
            pl.BlockSpec((None, 1, tn), lambda i, j: (layer, 0, nj + j)),
        ],
        out_specs=pl.BlockSpec((tm, tn), lambda i, j: (i, j)),
        out_shape=jax.ShapeDtypeStruct((m, D_FF), BF16),
        scratch_shapes=[pltpu.VMEM((HALO_ROWS + tm, k), BF16)],
        compiler_params=_params("parallel", "arbitrary"),
        name="ffn_up",
    )(x, x, gain.reshape(1, k), w_up, w_up, conv_w, conv_w, cb, cb)


def _flash_kernel(qa_ref, qb_ref, ka_ref, kb_ref, v_ref, o_ref, m_ref, acc_ref,
                  s0, s1, p0, p1, a0, a1):
    i = pl.program_id(1)
    t = FLASH_TILE
    sbuf, pbuf, abuf = (s0, s1), (p0, p1), (a0, a1)

    def scores(j, slot):
        start = pl.multiple_of(j * t, t)
        keys = jnp.concatenate([ka_ref[pl.ds(start, t), :], kb_ref[pl.ds(start, t), :]], axis=1)
        q = jnp.concatenate([qa_ref[...], qb_ref[...]], axis=1)
        sbuf[slot][...] = lax.dot_general(q, keys, _NT, preferred_element_type=F32)

    def softmax(slot, causal=False):
        s = sbuf[slot][...]
        if causal:
            row = lax.broadcasted_iota(jnp.int32, (t, t), 0)
            col = lax.broadcasted_iota(jnp.int32, (t, t), 1)
            s = jnp.where(col <= row, s, NEG_INF)
        m_prev = m_ref[...]
        m_new = jnp.maximum(m_prev, jnp.max(s, axis=1, keepdims=True))
        abuf[slot][...] = jnp.exp2(m_prev - m_new)
        pbuf[slot][...] = jnp.exp2(s - jnp.tile(m_new, (1, t // LANES))).astype(BF16)
        m_ref[...] = m_new

    def accumulate(j, slot):
        start = pl.multiple_of(j * t, t)
        vals = jnp.concatenate([v_ref[pl.ds(start, t), :], jnp.ones((t, LANES), BF16)], axis=1)
        acc_ref[...] = (jnp.tile(abuf[slot][...], (1, 2)) * acc_ref[...]
                        + jnp.dot(pbuf[slot][...], vals, preferred_element_type=F32))

    m_ref[...] = jnp.full(m_ref.shape, MASKED, F32)
    acc_ref[...] = jnp.zeros(acc_ref.shape, F32)
    scores(i, 0)
    scores(0, 1)
    softmax(0, causal=True)

    def pair(k, carry):
        scores(2 * k + 1, 0)
        accumulate(jnp.where(k == 0, i, 2 * k - 1), 0)
        softmax(1)
        scores(jnp.minimum(2 * k + 2, i - 1), 1)
        accumulate(2 * k, 1)
        softmax(0)
        return carry

    lax.fori_loop(0, i // 2, pair, 0)

    @pl.when(i % 2 == 0)
    def _():
        accumulate(jnp.where(i == 0, 0, i - 1), 0)

    @pl.when(i % 2 == 1)
    def _():
        accumulate(jnp.where(i == 1, i, i - 2), 0)
        softmax(1)
        accumulate(i - 1, 1)

    acc = acc_ref[...]
    o_ref[...] = (acc[:, :LANES] / acc[:, LANES:]).astype(o_ref.dtype)


def _flash_attention(qa, qa_map, qb, qb_map, ka, ka_map, kb, kb_map, v, v_map, *, heads, name):
    s = ka.shape[0]
    t = FLASH_TILE
    assert s % t == 0
    return pl.pallas_call(
        _flash_kernel,
        grid=(heads, s // t),
        in_specs=[
            pl.BlockSpec((t, LANES), qa_map),
            pl.BlockSpec((t, LANES), qb_map),
            pl.BlockSpec((s, LANES), ka_map),
            pl.BlockSpec((s, LANES), kb_map),
            pl.BlockSpec((s, LANES), v_map),
        ],
        out_specs=pl.BlockSpec((t, LANES), lambda h, i: (i, h)),
        out_shape=jax.ShapeDtypeStruct((s, heads * LANES), BF16),
        scratch_shapes=[
            pltpu.VMEM((t, LANES), F32),
            pltpu.VMEM((t, 2 * LANES), F32),
            pltpu.VMEM((t, t), F32), pltpu.VMEM((t, t), F32),
            pltpu.VMEM((t, t), BF16), pltpu.VMEM((t, t), BF16),
            pltpu.VMEM((t, LANES), F32), pltpu.VMEM((t, LANES), F32),
        ],
        compiler_params=_params("parallel", "arbitrary"),
        name=name,
    )(qa, qb, ka, kb, v)


def _moba_select_kernel(q_ref, k_ref, bias_ref, kmh_ref, kml_ref, *, nb):
    i = pl.program_id(1)
    bs = MOBA_BLOCK
    tq = q_ref.shape[0]

    @pl.when(i == 0)
    def _():
        kmh_ref[...] = jnp.zeros(kmh_ref.shape, F32)
        kml_ref[...] = jnp.zeros(kml_ref.shape, F32)
        for b in range(nb):
            km = jnp.mean(k_ref[b * bs:(b + 1) * bs, :].astype(F32), axis=0, keepdims=True)
            hi = km.astype(BF16).astype(F32)
            kmh_ref[b:b + 1, :] = hi
            kml_ref[b:b + 1, :] = km - hi

    nbp = kmh_ref.shape[0]
    q = q_ref[...]
    gate = (lax.dot_general(kmh_ref[...].astype(BF16), q, _NT, preferred_element_type=F32)
            + lax.dot_general(kml_ref[...].astype(BF16), q, _NT, preferred_element_type=F32))
    blk = lax.broadcasted_iota(jnp.int32, gate.shape, 0).astype(F32)
    pos = lax.broadcasted_iota(jnp.int32, gate.shape, 1)
    own = (i * (tq // bs) + jnp.right_shift(pos, bs.bit_length() - 1)).astype(F32)
    gate = jnp.where(blk < own, gate, NEG_INF)
    sel = jnp.zeros(gate.shape, F32)
    for r in range(min(MOBA_TOPK, nb)):
        mx = jnp.max(gate, axis=0, keepdims=True)
        first = jnp.min(jnp.where(gate == mx, blk, float(nbp)), axis=0, keepdims=True)
        pick = blk == first
        sel = jnp.maximum(sel, jnp.where(pick, jnp.where(own > r, 1.0, 0.0), 0.0))
        gate = jnp.where(pick, NEG_INF, gate)
    visible = jnp.where(blk == own, 1.0, sel)
    bias = jnp.where(visible > 0.5, 0.0, MASKED)
    bias = jnp.where(blk < nb, bias, 0.0)
    padded = jnp.concatenate([bias, jnp.zeros((LANES - nbp, tq), F32)], axis=0)
    bias_ref[...] = padded.T.astype(BF16)


def _moba_attention(qkv, *, tq=1024):
    s = qkv.shape[0]
    bs = MOBA_BLOCK
    assert s % bs == 0 and s % tq == 0 and tq % bs == 0
    nb = s // bs
    nbp = -(-nb // 8) * 8
    assert nbp < LANES
    bias = pl.pallas_call(
        functools.partial(_moba_select_kernel, nb=nb),
        grid=(N_HEADS, s // tq),
        in_specs=[
            pl.BlockSpec((tq, HEAD_DIM), lambda h, i: (i, h)),
            pl.BlockSpec((s, HEAD_DIM), lambda h, i: (0, N_HEADS + h)),
        ],
        out_specs=pl.BlockSpec((tq, LANES), lambda h, i: (i, h)),
        out_shape=jax.ShapeDtypeStruct((s, N_HEADS * LANES), BF16),
        scratch_shapes=[pltpu.VMEM((nbp, HEAD_DIM), F32), pltpu.VMEM((nbp, HEAD_DIM), F32)],
        compiler_params=_params("parallel", "arbitrary"),
        name="moba_select",
    )(qkv, qkv)
    key_block = (jnp.arange(s, dtype=jnp.int32) // bs)[:, None]
    onehot = (key_block == jnp.arange(LANES, dtype=jnp.int32)[None, :]).astype(BF16)
    return _flash_attention(
        qkv, lambda h, i: (i, h), bias, lambda h, i: (i, h),
        qkv, lambda h, i: (0, N_HEADS + h), onehot, lambda h, i: (0, 0),
        qkv, lambda h, i: (0, 2 * N_HEADS + h), heads=N_HEADS, name="moba_flash")


def _mla_down_kernel(x_ref, g_ref, w_ref, qn_ref, kvn_ref, cos_ref, sin_ref,
                     cq_ref, ckv_ref, kr_ref):
    xn = _rms_normalize(x_ref[...], g_ref[...]).astype(BF16)
    lat = jnp.dot(xn, w_ref[...], preferred_element_type=F32)
    q0, kv0, r0, rr0 = 0, MLA_Q_RANK, MLA_Q_RANK + MLA_KV_RANK, MLA_Q_RANK + MLA_KV_RANK + LANES
    cq_ref[...] = _rms_normalize(lat[:, q0:kv0], qn_ref[...]).astype(BF16)
    ckv_ref[...] = _rms_normalize(lat[:, kv0:r0], kvn_ref[...]).astype(BF16)
    kr_ref[...] = (lat[:, r0:rr0] * cos_ref[...] + lat[:, rr0:rr0 + LANES] * sin_ref[...]).astype(BF16)


def _mla_down(x, gain, w_ext, q_norm, kv_norm, cos_r, sin_r, *, tm=512):
    m, k = x.shape
    n = w_ext.shape[1]
    assert m % tm == 0
    row = lambda i: (i, 0)
    fixed = lambda i: (0, 0)
    return pl.pallas_call(
        _mla_down_kernel,
        grid=(m // tm,),
        in_specs=[
            pl.BlockSpec((tm, k), row),
            pl.BlockSpec((1, k), fixed),
            pl.BlockSpec((k, n), fixed),
            pl.BlockSpec((1, MLA_Q_RANK), fixed),
            pl.BlockSpec((1, MLA_KV_RANK), fixed),
            pl.BlockSpec((tm, LANES), row),
            pl.BlockSpec((tm, LANES), row),
        ],
        out_specs=[
            pl.BlockSpec((tm, MLA_Q_RANK), row),
            pl.BlockSpec((tm, MLA_KV_RANK), row),
            pl.BlockSpec((tm, LANES), row),
        ],
        out_shape=[
            jax.ShapeDtypeStruct((m, MLA_Q_RANK), BF16),
            jax.ShapeDtypeStruct((m, MLA_KV_RANK), BF16),
            jax.ShapeDtypeStruct((m, LANES), BF16),
        ],
        compiler_params=_params("parallel"),
        name="mla_down",
    )(x, gain.reshape(1, k), w_ext, q_norm.reshape(1, -1), kv_norm.reshape(1, -1), cos_r, sin_r)


def _mla_up_kernel(cq_ref, ckv_ref, wq_ref, wkv_ref, cos_ref, sin_ref, q_ref, kv_ref, *, q_scale):
    cq = cq_ref[...]
    cos = cos_ref[...] * q_scale
    sin = sin_ref[...] * q_scale
    per_head = MLA_NOPE_DIM + 2 * LANES
    for h in range(MLA_HEADS):
        r = jnp.dot(cq, wq_ref[:, h * per_head:(h + 1) * per_head], preferred_element_type=F32)
        rope = r[:, MLA_NOPE_DIM:MLA_NOPE_DIM + LANES] * cos + r[:, MLA_NOPE_DIM + LANES:] * sin
        q_ref[:, 2 * h * LANES:(2 * h + 1) * LANES] = (r[:, :MLA_NOPE_DIM] * q_scale).astype(BF16)
        q_ref[:, (2 * h + 1) * LANES:(2 * h + 2) * LANES] = rope.astype(BF16)
    kv_ref[...] = jnp.dot(ckv_ref[...], wkv_ref[...], preferred_element_type=F32).astype(BF16)


def _mla_up(cq, ckv, wq_ext, wkv, cos_r, sin_r, *, tm=512):
    m = cq.shape[0]
    assert m % tm == 0 and MLA_NOPE_DIM == LANES
    row = lambda i: (i, 0)
    fixed = lambda i: (0, 0)
    nq = MLA_HEADS * 2 * LANES
    nkv = wkv.shape[1]
    kern = functools.partial(_mla_up_kernel,
                             q_scale=(MLA_NOPE_DIM + MLA_ROPE_DIM) ** -0.5 * LOG2E)
    return pl.pallas_call(
        kern,
        grid=(m // tm,),
        in_specs=[
            pl.BlockSpec((tm, MLA_Q_RANK), row),
            pl.BlockSpec((tm, MLA_KV_RANK), row),
            pl.BlockSpec(wq_ext.shape, fixed),
            pl.BlockSpec(wkv.shape, fixed),
            pl.BlockSpec((tm, LANES), row),
            pl.BlockSpec((tm, LANES), row),
        ],
        out_specs=[pl.BlockSpec((tm, nq), row), pl.BlockSpec((tm, nkv), row)],
        out_shape=[jax.ShapeDtypeStruct((m, nq), BF16), jax.ShapeDtypeStruct((m, nkv), BF16)],
        compiler_params=_params("parallel"),
        name="mla_up",
    )(cq, ckv, wq_ext, wkv, cos_r, sin_r)


def _mla_attention(q_full, kv, k_rope):
    return _flash_attention(
        q_full, lambda h, i: (i, 2 * h), q_full, lambda h, i: (i, 2 * h + 1),
        kv, lambda h, i: (0, h), k_rope, lambda h, i: (0, 0),
        kv, lambda h, i: (0, MLA_HEADS + h), heads=MLA_HEADS, name="mla_flash")


def _dilated_kernel(q_ref, k_ref, v_ref, o_ref, qf, kf, vf, num, den, mx):
    band = DIL_BAND
    s_len = q_ref.shape[0]
    qf[...] = q_ref[...].astype(F32)
    kf[...] = k_ref[...].astype(F32)
    vf[...] = v_ref[...].astype(F32)

    row = lax.broadcasted_iota(jnp.int32, (band, 2 * band), 0)
    col = lax.broadcasted_iota(jnp.int32, (band, 2 * band), 1)
    outside = jnp.where(col < band, row - col, col - band - row)
    bias_inner = jnp.where(outside <= 0, 0.0, NEG_INF)
    bias_first = jnp.where(col <= row, 0.0, NEG_INF)
    ones = jnp.ones((2 * band, LANES), BF16)

    def rows(start, n, dil):
        return pl.ds(start, n) if dil == 1 else pl.ds(start, n, stride=dil)

    def block(dil, r, n, bias, first_pattern):
        q_start = n * band * dil + r
        k_start = jnp.maximum(n - 1, 0) * band * dil + r
        q = qf[rows(q_start, band, dil), :].astype(BF16)
        k = kf[rows(k_start, 2 * band, dil), :].astype(BF16)
        v = vf[rows(k_start, 2 * band, dil), :].astype(BF16)
        s = lax.dot_general(q, k, _NT, preferred_element_type=F32) + bias
        m = jnp.max(s, axis=1, keepdims=True)
        p = jnp.exp2(s - m).astype(BF16)
        pv = jnp.dot(p, jnp.concatenate([v, ones], axis=1), preferred_element_type=F32)
        m_blk = jnp.broadcast_to(m, (band, LANES))
        dst = rows(q_start, band, dil)
        if first_pattern:
            num[dst, :] = pv[:, :LANES]
            den[dst, :] = pv[:, LANES:]
            mx[dst, :] = m_blk
        else:
            m_old = mx[dst, :]
            m_new = jnp.maximum(m_old, m_blk)
            a_old = jnp.exp2(m_old - m_new)
            a_blk = jnp.exp2(m_blk - m_new)
            num[dst, :] = a_old * num[dst, :] + a_blk * pv[:, :LANES]
            den[dst, :] = a_old * den[dst, :] + a_blk * pv[:, LANES:]
            mx[dst, :] = m_new

    for idx, (_, dil) in enumerate(DIL_PATTERNS):
        nb = s_len // dil // band
        aligned = math.gcd(nb, DIL_UNROLL)

        def step(t, carry, dil=dil, nb=nb, aligned=aligned, first=(idx == 0)):
            for u in range(DIL_UNROLL):
                flat = t * DIL_UNROLL + u
                r, n = flat // nb, flat % nb
                bias = jnp.where(n == 0, bias_first, bias_inner) if u % aligned == 0 else bias_inner
                block(dil, r, n, bias, first)
            return carry

        lax.fori_loop(0, dil * nb // DIL_UNROLL, step, 0)

    o_ref[...] = (num[...] / den[...]).astype(o_ref.dtype)


def _dilated_attention(qkv):
    s = qkv.shape[0]
    for window, dil in DIL_PATTERNS:
        assert window // dil == DIL_BAND
        nb = s // dil // DIL_BAND
        assert s % (dil * DIL_BAND) == 0 and nb >= 2 and (dil * nb) % DIL_UNROLL == 0
    return pl.pallas_call(
        _dilated_kernel,
        grid=(N_HEADS,),
        in_specs=[
            pl.BlockSpec((s, HEAD_DIM), lambda h: (0, h)),
            pl.BlockSpec((s, HEAD_DIM), lambda h: (0, N_HEADS + h)),
            pl.BlockSpec((s, HEAD_DIM), lambda h: (0, 2 * N_HEADS + h)),
        ],
        out_specs=pl.BlockSpec((s, HEAD_DIM), lambda h: (0, h)),
        out_shape=jax.ShapeDtypeStruct((s, N_HEADS * HEAD_DIM), BF16),
        scratch_shapes=[pltpu.VMEM((s, HEAD_DIM), F32)] * 3 + [pltpu.VMEM((s, LANES), F32)] * 3,
        compiler_params=_params("parallel"),
        name="dilated_attention",
    )(qkv, qkv, qkv)


def _final_norm_kernel(x_ref, g_ref, o_ref):
    o_ref[...] = _rms_normalize(x_ref[...], g_ref[...])


def _final_norm(x, gain, *, tm=512):
    m, k = x.shape
    assert m % tm == 0
    return pl.pallas_call(
        _final_norm_kernel,
        grid=(m // tm,),
        in_specs=[pl.BlockSpec((tm, k), lambda i: (i, 0)), pl.BlockSpec((1, k), lambda i: (0, 0))],
        out_specs=pl.BlockSpec((tm, k), lambda i: (i, 0)),
        out_shape=jax.ShapeDtypeStruct((m, k), F32),
        compiler_params=_params("parallel"),
        name="final_norm",
    )(x, gain.reshape(1, k))


def _rope_tables(n_pos, dim):
    inv = 1.0 / (ROPE_THETA ** (jnp.arange(0, dim, 2, dtype=F32) / dim))
    ang = jnp.arange(n_pos, dtype=F32)[:, None] * inv[None, :]
    return jnp.cos(ang), jnp.sin(ang)


def _rotate_half_columns(w):
    half = w.shape[1] // 2
    return jnp.concatenate([-w[:, half:], w[:, :half]], axis=1)


def _pad_cols(w, width):
    return jnp.pad(w, ((0, 0), (0, width - w.shape[1])))


def _mla_weights(w_down, w_uq, w_ukv):
    lat = MLA_Q_RANK + MLA_KV_RANK
    w_rope = w_down[:, lat:]
    w_down_ext = jnp.concatenate(
        [w_down[:, :lat], _pad_cols(w_rope, LANES), _pad_cols(_rotate_half_columns(w_rope), LANES)],
        axis=1).astype(BF16)
    wq = w_uq.reshape(MLA_Q_RANK, MLA_HEADS, MLA_NOPE_DIM + MLA_ROPE_DIM)
    heads = []
    for h in range(MLA_HEADS):
        w_r = wq[:, h, MLA_NOPE_DIM:]
        heads += [wq[:, h, :MLA_NOPE_DIM], _pad_cols(w_r, LANES),
                  _pad_cols(_rotate_half_columns(w_r), LANES)]
    wq_ext = jnp.concatenate(heads, axis=1).astype(BF16)
    wkv = w_ukv.reshape(MLA_KV_RANK, MLA_HEADS, MLA_NOPE_DIM + MLA_V_DIM)
    wkv = jnp.concatenate([wkv[:, :, :MLA_NOPE_DIM].reshape(MLA_KV_RANK, -1),
                           wkv[:, :, MLA_NOPE_DIM:].reshape(MLA_KV_RANK, -1)], axis=1).astype(BF16)
    return w_down_ext, wq_ext, wkv


def kernel(x, attn_norm, ffn_norm, final_norm, moba_w_qkv, moba_w_o, mla_w_down, mla_q_norm,
           mla_kv_norm, mla_w_uq, mla_w_ukv, mla_w_o, dil_w_qkv, dil_w_o, ffn_w_up, ffn_conv_w,
           ffn_conv_b, ffn_w_down):
    b, s, d = x.shape
    assert b == 1 and d == D_MODEL
    cos, sin = _rope_tables(s, HEAD_DIM)
    cos_full = jnp.concatenate([cos, cos], axis=1)
    sin_signed = jnp.concatenate([-sin, sin], axis=1)
    cos_r, sin_r = _rope_tables(s, MLA_ROPE_DIM)
    cos_r = _pad_cols(jnp.concatenate([cos_r, cos_r], axis=1), LANES)
    sin_r = _pad_cols(jnp.concatenate([sin_r, sin_r], axis=1), LANES)

    qkv_w = {0: moba_w_qkv.astype(BF16), 2: dil_w_qkv.astype(BF16)}
    out_w = {0: moba_w_o.astype(BF16), 1: mla_w_o.astype(BF16), 2: dil_w_o.astype(BF16)}
    down_w = ffn_w_down.astype(BF16)

    h = x.reshape(s, d)
    for i in range(DEPTH):
        j = i // N_MIXERS
        kind = i % N_MIXERS
        if kind == 1:
            w_down_ext, wq_ext, wkv = _mla_weights(mla_w_down[j], mla_w_uq[j], mla_w_ukv[j])
            cq, ckv, k_rope = _mla_down(h, attn_norm[i], w_down_ext, mla_q_norm[j], mla_kv_norm[j],
                                        cos_r, sin_r)
            q_full, kv = _mla_up(cq, ckv, wq_ext, wkv, cos_r, sin_r)
            mix = _mla_attention(q_full, kv, k_rope)
        else:
            qkv = _qkv_projection(h, attn_norm[i], qkv_w[kind], j, cos_full, sin_signed)
            mix = _moba_attention(qkv) if kind == 0 else _dilated_attention(qkv)
        h = _matmul_residual(mix, out_w[kind], j, h, tn=1024)
        act = _ffn_up(h, ffn_norm[i], ffn_w_up, ffn_conv_w, ffn_conv_b, i)
        h = _matmul_residual(act, down_w, i, h, tn=512)
    return _final_norm(h, final_norm).reshape(b, s, d)
```

```python
import functools
import math

import jax
import jax.numpy as jnp
from jax import lax
from jax.experimental import pallas as pl
from jax.experimental.pallas import tpu as pltpu

F32 = jnp.float32
BF16 = jnp.bfloat16

D_MODEL = 2048
DEPTH = 4
N_MIXERS = 3
ROPE_THETA = 10000.0
NORM_EPS = 1e-6

N_HEADS = 16
HEAD_DIM = D_MODEL // N_HEADS
MOBA_BLOCK = 256
MOBA_TOPK = 3

MLA_HEADS = 16
MLA_Q_RANK = 512
MLA_KV_RANK = 512
MLA_NOPE_DIM = 128
MLA_ROPE_DIM = 64
MLA_V_DIM = 128

DIL_PATTERNS = ((128, 1), (512, 4), (2048, 16))
DIL_BAND = 128
DIL_UNROLL = 8

D_FF = 5632
CONV_WIDTH = 3

LANES = 128
HALO_ROWS = 16
VMEM_LIMIT_BYTES = 56 * 1024 * 1024
FLASH_TILE = 512
FLASH_PAIRS = 2

LOG2E = math.log2(math.e)
NEG_INF = float("-inf")
MASKED = -1e30

_NT = (((1,), (1,)), ((), ()))


def _params(*semantics):
    return pltpu.CompilerParams(dimension_semantics=semantics,
                                vmem_limit_bytes=VMEM_LIMIT_BYTES)


def _rms_normalize(x, gain):
    ms = jnp.mean(x * x, axis=-1, keepdims=True)
    return x * lax.rsqrt(ms + NORM_EPS) * gain


def _norm_matmul_kernel(x_ref, g_ref, w_ref, cos_ref, sin_ref, o_ref, xn_ref, *,
                        q_tiles, rope_tiles, q_scale):
    j = pl.program_id(1)

    @pl.when(j == 0)
    def _():
        xn_ref[...] = _rms_normalize(x_ref[...], g_ref[...]).astype(BF16)

    acc = jnp.dot(xn_ref[...], w_ref[...].astype(BF16), preferred_element_type=F32)

    @pl.when(j < rope_tiles)
    def _():
        col_scale = jnp.where(j < q_tiles, q_scale, 1.0)
        cos = cos_ref[...] * col_scale
        sin = sin_ref[...] * col_scale
        for c in range(acc.shape[1] // HEAD_DIM):
            xh = acc[:, c * HEAD_DIM:(c + 1) * HEAD_DIM]
            roped = xh * cos + pltpu.roll(xh, HEAD_DIM // 2, axis=1) * sin
            o_ref[:, c * HEAD_DIM:(c + 1) * HEAD_DIM] = roped.astype(o_ref.dtype)

    @pl.when(j >= rope_tiles)
    def _():
        o_ref[...] = acc.astype(o_ref.dtype)


def _qkv_projection(x, gain, w, layer, cos_full, sin_signed, *, tm=1024, tn=1024):
    m, k = x.shape
    n = w.shape[2]
    width = N_HEADS * HEAD_DIM
    assert m % tm == 0 and n % tn == 0 and width % tn == 0 and n == 3 * width
    kern = functools.partial(_norm_matmul_kernel, q_tiles=width // tn, rope_tiles=2 * width // tn,
                             q_scale=HEAD_DIM ** -0.5 * LOG2E)
    return pl.pallas_call(
        kern,
        grid=(m // tm, n // tn),
        in_specs=[
            pl.BlockSpec((tm, k), lambda i, j: (i, 0)),
            pl.BlockSpec((1, k), lambda i, j: (0, 0)),
            pl.BlockSpec((None, k, tn), lambda i, j: (layer, 0, j)),
            pl.BlockSpec((tm, HEAD_DIM), lambda i, j: (i, 0)),
            pl.BlockSpec((tm, HEAD_DIM), lambda i, j: (i, 0)),
        ],
        out_specs=pl.BlockSpec((tm, tn), lambda i, j: (i, j)),
        out_shape=jax.ShapeDtypeStruct((m, n), BF16),
        scratch_shapes=[pltpu.VMEM((tm, k), BF16)],
        compiler_params=_params("parallel", "arbitrary"),
        name="qkv_projection",
    )(x, gain.reshape(1, k), w, cos_full, sin_signed)


def _matmul_residual_kernel(a_ref, w_ref, r_ref, o_ref):
    o_ref[...] = r_ref[...] + jnp.dot(a_ref[...], w_ref[...].astype(BF16), preferred_element_type=F32)


def _matmul_residual(a, w, layer, res, *, tm=1024, tn=512):
    m, k = a.shape
    n = w.shape[2]
    assert m % tm == 0 and n % tn == 0
    return pl.pallas_call(
        _matmul_residual_kernel,
        grid=(m // tm, n // tn),
        in_specs=[
            pl.BlockSpec((tm, k), lambda i, j: (i, 0)),
            pl.BlockSpec((None, k, tn), lambda i, j: (layer, 0, j)),
            pl.BlockSpec((tm, tn), lambda i, j: (i, j)),
        ],
        out_specs=pl.BlockSpec((tm, tn), lambda i, j: (i, j)),
        out_shape=jax.ShapeDtypeStruct((m, n), F32),
        compiler_params=_params("parallel", "arbitrary"),
        name="matmul_residual",
    )(a, w, res)


def _ffn_up_kernel(x_ref, halo_ref, g_ref, wg_ref, wv_ref, cwg_ref, cwv_ref, cbg_ref, cbv_ref,
                   o_ref, xn_ref):
    i = pl.program_id(0)
    j = pl.program_id(1)

    @pl.when(j == 0)
    def _():
        gain = g_ref[...]
        halo = _rms_normalize(halo_ref[...], gain)
        halo = jnp.where(i == 0, 0.0, halo)
        xn_ref[0:HALO_ROWS, :] = halo.astype(BF16)
        xn_ref[HALO_ROWS:, :] = _rms_normalize(x_ref[...], gain).astype(BF16)

    def conv(w_ref, cw_ref, cb_ref):
        h = jnp.dot(xn_ref[...], w_ref[...].astype(BF16), preferred_element_type=F32)
        cw = cw_ref[...]
        y = (pltpu.roll(h, 2, axis=0) * cw[0:1, :] + pltpu.roll(h, 1, axis=0) * cw[1:2, :]
             + h * cw[2:3, :])
        return y[HALO_ROWS:, :] + cb_ref[...]

    gate = conv(wg_ref, cwg_ref, cbg_ref)
    val = conv(wv_ref, cwv_ref, cbv_ref)
    o_ref[...] = (gate * (1.0 / (1.0 + jnp.exp(-gate))) * val).astype(o_ref.dtype)


def _ffn_up(x, gain, w_up, conv_w, conv_b, layer, *, tm=1024, tn=512):
    m, k = x.shape
    assert m % tm == 0 and D_FF % tn == 0 and tm % HALO_ROWS == 0
    nj = D_FF // tn
    halo_blocks = tm // HALO_ROWS
    cb = conv_b.reshape(conv_b.shape[0], 1, 2 * D_FF)
    return pl.pallas_call(
        _ffn_up_kernel,
        grid=(m // tm, nj),
        in_specs=[
            pl.BlockSpec((tm, k), lambda i, j: (i, 0)),
            pl.BlockSpec((HALO_ROWS, k), lambda i, j: (jnp.maximum(i * halo_blocks - 1, 0), 0)),
            pl.BlockSpec((1, k), lambda i, j: (0, 0)),
            pl.BlockSpec((None, k, tn), lambda i, j: (layer, 0, j)),
            pl.BlockSpec((None, k, tn), lambda i, j: (layer, 0, nj + j)),
            pl.BlockSpec((None, CONV_WIDTH, tn), lambda i, j: (layer, 0, j)),
            pl.BlockSpec((None, CONV_WIDTH, tn), lambda i, j: (layer, 0, nj + j)),
            pl.BlockSpec((None, 1, tn), lambda i, j: (layer, 0, j)),
            pl.BlockSpec((None, 1, tn), lambda i, j: (layer, 0, nj + j)),
        ],
        out_specs=pl.BlockSpec((tm, tn), lambda i, j: (i, j)),
        out_shape=jax.ShapeDtypeStruct((m, D_FF), BF16),
        scratch_shapes=[pltpu.VMEM((HALO_ROWS + tm, k), BF16)],
        compiler_params=_params("parallel", "arbitrary"),
        name="ffn_up",
    )(x, x, gain.reshape(1, k), w_up, w_up, conv_w, conv_w, cb, cb)


def _flash_kernel(qa_ref, qb_ref, ka_ref, kb_ref, v_ref, o_ref, m_ref, acc_ref,
                  s0, s1, p0, p1, a0, a1):
    i = pl.program_id(1)
    t = FLASH_TILE
    sbuf, pbuf, abuf = (s0, s1), (p0, p1), (a0, a1)

    def scores(j, slot):
        start = pl.multiple_of(j * t, t)
        keys = jnp.concatenate([ka_ref[pl.ds(start, t), :], kb_ref[pl.ds(start, t), :]], axis=1)
        q = jnp.concatenate([qa_ref[...], qb_ref[...]], axis=1)
        sbuf[slot][...] = lax.dot_general(q, keys, _NT, preferred_element_type=F32)

    def softmax(slot, causal=False):
        s = sbuf[slot][...]
        if causal:
            row = lax.broadcasted_iota(jnp.int32, (t, t), 0)
            col = lax.broadcasted_iota(jnp.int32, (t, t), 1)
            s = jnp.where(col <= row, s, NEG_INF)
        m_prev = m_ref[...]
        m_new = jnp.maximum(m_prev, jnp.max(s, axis=1, keepdims=True))
        abuf[slot][...] = jnp.exp2(m_prev - m_new)
        pbuf[slot][...] = jnp.exp2(s - jnp.tile(m_new, (1, t // LANES))).astype(BF16)
        m_ref[...] = m_new

    def accumulate(j, slot):
        start = pl.multiple_of(j * t, t)
        vals = jnp.concatenate([v_ref[pl.ds(start, t), :], jnp.ones((t, LANES), BF16)], axis=1)
        acc_ref[...] = (jnp.tile(abuf[slot][...], (1, 2)) * acc_ref[...]
                        + jnp.dot(pbuf[slot][...], vals, preferred_element_type=F32))

    m_ref[...] = jnp.full(m_ref.shape, MASKED, F32)
    acc_ref[...] = jnp.zeros(acc_ref.shape, F32)
    scores(i, 0)
    scores(0, 1)
    softmax(0, causal=True)

    def pair(k):
        scores(2 * k + 1, 0)
        accumulate(jnp.where(k == 0, i, 2 * k - 1), 0)
        softmax(1)
        scores(jnp.minimum(2 * k + 2, i - 1), 1)
        accumulate(2 * k, 1)
        softmax(0)

    pairs = i // 2

    def trip(step, carry):
        for u in range(FLASH_PAIRS):
            pair(step * FLASH_PAIRS + u)
        return carry

    lax.fori_loop(0, pairs // FLASH_PAIRS, trip, 0)

    def leftover(k, carry):
        pair(k)
        return carry

    lax.fori_loop(pairs - pairs % FLASH_PAIRS, pairs, leftover, 0)

    @pl.when(i % 2 == 0)
    def _():
        accumulate(jnp.where(i == 0, 0, i - 1), 0)

    @pl.when(i % 2 == 1)
    def _():
        accumulate(jnp.where(i == 1, i, i - 2), 0)
        softmax(1)
        accumulate(i - 1, 1)

    acc = acc_ref[...]
    o_ref[...] = (acc[:, :LANES] / acc[:, LANES:]).astype(o_ref.dtype)


def _flash_attention(qa, qa_map, qb, qb_map, ka, ka_map, kb, kb_map, v, v_map, *, heads, name):
    s = ka.shape[0]
    t = FLASH_TILE
    assert s % t == 0
    return pl.pallas_call(
        _flash_kernel,
        grid=(heads, s // t),
        in_specs=[
            pl.BlockSpec((t, LANES), qa_map),
            pl.BlockSpec((t, LANES), qb_map),
            pl.BlockSpec((s, LANES), ka_map),
            pl.BlockSpec((s, LANES), kb_map),
            pl.BlockSpec((s, LANES), v_map),
        ],
        out_specs=pl.BlockSpec((t, LANES), lambda h, i: (i, h)),
        out_shape=jax.ShapeDtypeStruct((s, heads * LANES), BF16),
        scratch_shapes=[
            pltpu.VMEM((t, LANES), F32),
            pltpu.VMEM((t, 2 * LANES), F32),
            pltpu.VMEM((t, t), F32), pltpu.VMEM((t, t), F32),
            pltpu.VMEM((t, t), BF16), pltpu.VMEM((t, t), BF16),
            pltpu.VMEM((t, LANES), F32), pltpu.VMEM((t, LANES), F32),
        ],
        compiler_params=_params("parallel", "arbitrary"),
        name=name,
    )(qa, qb, ka, kb, v)


def _moba_select_kernel(q_ref, k_ref, bias_ref, kmh_ref, kml_ref, *, nb):
    i = pl.program_id(1)
    bs = MOBA_BLOCK
    tq = q_ref.shape[0]

    @pl.when(i == 0)
    def _():
        kmh_ref[...] = jnp.zeros(kmh_ref.shape, F32)
        kml_ref[...] = jnp.zeros(kml_ref.shape, F32)
        for b in range(nb):
            km = jnp.mean(k_ref[b * bs:(b + 1) * bs, :].astype(F32), axis=0, keepdims=True)
            hi = km.astype(BF16).astype(F32)
            kmh_ref[b:b + 1, :] = hi
            kml_ref[b:b + 1, :] = km - hi

    nbp = kmh_ref.shape[0]
    q = q_ref[...]
    gate = (lax.dot_general(kmh_ref[...].astype(BF16), q, _NT, preferred_element_type=F32)
            + lax.dot_general(kml_ref[...].astype(BF16), q, _NT, preferred_element_type=F32))
    blk = lax.broadcasted_iota(jnp.int32, gate.shape, 0).astype(F32)
    pos = lax.broadcasted_iota(jnp.int32, gate.shape, 1)
    own = (i * (tq // bs) + jnp.right_shift(pos, bs.bit_length() - 1)).astype(F32)
    gate = jnp.where(blk < own, gate, NEG_INF)
    sel = jnp.zeros(gate.shape, F32)
    for r in range(min(MOBA_TOPK, nb)):
        mx = jnp.max(gate, axis=0, keepdims=True)
        first = jnp.min(jnp.where(gate == mx, blk, float(nbp)), axis=0, keepdims=True)
        pick = blk == first
        sel = jnp.maximum(sel, jnp.where(pick, jnp.where(own > r, 1.0, 0.0), 0.0))
        gate = jnp.where(pick, NEG_INF, gate)
    visible = jnp.where(blk == own, 1.0, sel)
    bias = jnp.where(visible > 0.5, 0.0, MASKED)
    bias = jnp.where(blk < nb, bias, 0.0)
    padded = jnp.concatenate([bias, jnp.zeros((LANES - nbp, tq), F32)], axis=0)
    bias_ref[...] = padded.T.astype(BF16)


def _moba_attention(qkv, *, tq=1024):
    s = qkv.shape[0]
    bs = MOBA_BLOCK
    assert s % bs == 0 and s % tq == 0 and tq % bs == 0
    nb = s // bs
    nbp = -(-nb // 8) * 8
    assert nbp < LANES
    bias = pl.pallas_call(
        functools.partial(_moba_select_kernel, nb=nb),
        grid=(N_HEADS, s // tq),
        in_specs=[
            pl.BlockSpec((tq, HEAD_DIM), lambda h, i: (i, h)),
            pl.BlockSpec((s, HEAD_DIM), lambda h, i: (0, N_HEADS + h)),
        ],
        out_specs=pl.BlockSpec((tq, LANES), lambda h, i: (i, h)),
        out_shape=jax.ShapeDtypeStruct((s, N_HEADS * LANES), BF16),
        scratch_shapes=[pltpu.VMEM((nbp, HEAD_DIM), F32), pltpu.VMEM((nbp, HEAD_DIM), F32)],
        compiler_params=_params("parallel", "arbitrary"),
        name="moba_select",
    )(qkv, qkv)
    key_block = (jnp.arange(s, dtype=jnp.int32) // bs)[:, None]
    onehot = (key_block == jnp.arange(LANES, dtype=jnp.int32)[None, :]).astype(BF16)
    return _flash_attention(
        qkv, lambda h, i: (i, h), bias, lambda h, i: (i, h),
        qkv, lambda h, i: (0, N_HEADS + h), onehot, lambda h, i: (0, 0),
        qkv, lambda h, i: (0, 2 * N_HEADS + h), heads=N_HEADS, name="moba_flash")


def _mla_down_kernel(x_ref, g_ref, w_ref, qn_ref, kvn_ref, cos_ref, sin_ref,
                     cq_ref, ckv_ref, kr_ref):
    xn = _rms_normalize(x_ref[...], g_ref[...]).astype(BF16)
    lat = jnp.dot(xn, w_ref[...], preferred_element_type=F32)
    q0, kv0, r0, rr0 = 0, MLA_Q_RANK, MLA_Q_RANK + MLA_KV_RANK, MLA_Q_RANK + MLA_KV_RANK + LANES
    cq_ref[...] = _rms_normalize(lat[:, q0:kv0], qn_ref[...]).astype(BF16)
    ckv_ref[...] = _rms_normalize(lat[:, kv0:r0], kvn_ref[...]).astype(BF16)
    kr_ref[...] = (lat[:, r0:rr0] * cos_ref[...] + lat[:, rr0:rr0 + LANES] * sin_ref[...]).astype(BF16)


def _mla_down(x, gain, w_ext, q_norm, kv_norm, cos_r, sin_r, *, tm=512):
    m, k = x.shape
    n = w_ext.shape[1]
    assert m % tm == 0
    row = lambda i: (i, 0)
    fixed = lambda i: (0, 0)
    return pl.pallas_call(
        _mla_down_kernel,
        grid=(m // tm,),
        in_specs=[
            pl.BlockSpec((tm, k), row),
            pl.BlockSpec((1, k), fixed),
            pl.BlockSpec((k, n), fixed),
            pl.BlockSpec((1, MLA_Q_RANK), fixed),
            pl.BlockSpec((1, MLA_KV_RANK), fixed),
            pl.BlockSpec((tm, LANES), row),
            pl.BlockSpec((tm, LANES), row),
        ],
        out_specs=[
            pl.BlockSpec((tm, MLA_Q_RANK), row),
            pl.BlockSpec((tm, MLA_KV_RANK), row),
            pl.BlockSpec((tm, LANES), row),
        ],
        out_shape=[
            jax.ShapeDtypeStruct((m, MLA_Q_RANK), BF16),
            jax.ShapeDtypeStruct((m, MLA_KV_RANK), BF16),
            jax.ShapeDtypeStruct((m, LANES), BF16),
        ],
        compiler_params=_params("parallel"),
        name="mla_down",
    )(x, gain.reshape(1, k), w_ext, q_norm.reshape(1, -1), kv_norm.reshape(1, -1), cos_r, sin_r)


def _mla_up_kernel(cq_ref, ckv_ref, wq_ref, wkv_ref, cos_ref, sin_ref, q_ref, kv_ref, *, q_scale):
    cq = cq_ref[...]
    cos = cos_ref[...] * q_scale
    sin = sin_ref[...] * q_scale
    per_head = MLA_NOPE_DIM + 2 * LANES
    for h in range(MLA_HEADS):
        r = jnp.dot(cq, wq_ref[:, h * per_head:(h + 1) * per_head], preferred_element_type=F32)
        rope = r[:, MLA_NOPE_DIM:MLA_NOPE_DIM + LANES] * cos + r[:, MLA_NOPE_DIM + LANES:] * sin
        q_ref[:, 2 * h * LANES:(2 * h + 1) * LANES] = (r[:, :MLA_NOPE_DIM] * q_scale).astype(BF16)
        q_ref[:, (2 * h + 1) * LANES:(2 * h + 2) * LANES] = rope.astype(BF16)
    kv_ref[...] = jnp.dot(ckv_ref[...], wkv_ref[...], preferred_element_type=F32).astype(BF16)


def _mla_up(cq, ckv, wq_ext, wkv, cos_r, sin_r, *, tm=512):
    m = cq.shape[0]
    assert m % tm == 0 and MLA_NOPE_DIM == LANES
    row = lambda i: (i, 0)
    fixed = lambda i: (0, 0)
    nq = MLA_HEADS * 2 * LANES
    nkv = wkv.shape[1]
    kern = functools.partial(_mla_up_kernel,
                             q_scale=(MLA_NOPE_DIM + MLA_ROPE_DIM) ** -0.5 * LOG2E)
    return pl.pallas_call(
        kern,
        grid=(m // tm,),
        in_specs=[
            pl.BlockSpec((tm, MLA_Q_RANK), row),
            pl.BlockSpec((tm, MLA_KV_RANK), row),
            pl.BlockSpec(wq_ext.shape, fixed),
            pl.BlockSpec(wkv.shape, fixed),
            pl.BlockSpec((tm, LANES), row),
            pl.BlockSpec((tm, LANES), row),
        ],
        out_specs=[pl.BlockSpec((tm, nq), row), pl.BlockSpec((tm, nkv), row)],
        out_shape=[jax.ShapeDtypeStruct((m, nq), BF16), jax.ShapeDtypeStruct((m, nkv), BF16)],
        compiler_params=_params("parallel"),
        name="mla_up",
    )(cq, ckv, wq_ext, wkv, cos_r, sin_r)


def _mla_attention(q_full, kv, k_rope):
    return _flash_attention(
        q_full, lambda h, i: (i, 2 * h), q_full, lambda h, i: (i, 2 * h + 1),
        kv, lambda h, i: (0, h), k_rope, lambda h, i: (0, 0),
        kv, lambda h, i: (0, MLA_HEADS + h), heads=MLA_HEADS, name="mla_flash")


def _dilated_kernel(q_ref, k_ref, v_ref, o_ref, qf, kf, vf, num, den, mx):
    band = DIL_BAND
    s_len = q_ref.shape[0]
    qf[...] = q_ref[...].astype(F32)
    kf[...] = k_ref[...].astype(F32)
    vf[...] = v_ref[...].astype(F32)

    row = lax.broadcasted_iota(jnp.int32, (band, 2 * band), 0)
    col = lax.broadcasted_iota(jnp.int32, (band, 2 * band), 1)
    outside = jnp.where(col < band, row - col, col - band - row)
    bias_inner = jnp.where(outside <= 0, 0.0, NEG_INF)
    bias_first = jnp.where(col <= row, 0.0, NEG_INF)
    ones = jnp.ones((2 * band, LANES), BF16)

    def rows(start, n, dil):
        return pl.ds(start, n) if dil == 1 else pl.ds(start, n, stride=dil)

    def block(dil, r, n, bias, first_pattern):
        q_start = n * band * dil + r
        k_start = jnp.maximum(n - 1, 0) * band * dil + r
        q = qf[rows(q_start, band, dil), :].astype(BF16)
        k = kf[rows(k_start, 2 * band, dil), :].astype(BF16)
        v = vf[rows(k_start, 2 * band, dil), :].astype(BF16)
        s = lax.dot_general(q, k, _NT, preferred_element_type=F32) + bias
        m = jnp.max(s, axis=1, keepdims=True)
        p = jnp.exp2(s - m).astype(BF16)
        pv = jnp.dot(p, jnp.concatenate([v, ones], axis=1), preferred_element_type=F32)
        m_blk = jnp.broadcast_to(m, (band, LANES))
        dst = rows(q_start, band, dil)
        if first_pattern:
            num[dst, :] = pv[:, :LANES]
            den[dst, :] = pv[:, LANES:]
            mx[dst, :] = m_blk
        else:
            m_old = mx[dst, :]
            m_new = jnp.maximum(m_old, m_blk)
            a_old = jnp.exp2(m_old - m_new)
            a_blk = jnp.exp2(m_blk - m_new)
            num[dst, :] = a_old * num[dst, :] + a_blk * pv[:, :LANES]
            den[dst, :] = a_old * den[dst, :] + a_blk * pv[:, LANES:]
            mx[dst, :] = m_new

    for idx, (_, dil) in enumerate(DIL_PATTERNS):
        nb = s_len // dil // band
        aligned = math.gcd(nb, DIL_UNROLL)

        def step(t, carry, dil=dil, nb=nb, aligned=aligned, first=(idx == 0)):
            for u in range(DIL_UNROLL):
                flat = t * DIL_UNROLL + u
                r, n = flat // nb, flat % nb
                bias = jnp.where(n == 0, bias_first, bias_inner) if u % aligned == 0 else bias_inner
                block(dil, r, n, bias, first)
            return carry

        lax.fori_loop(0, dil * nb // DIL_UNROLL, step, 0)

    o_ref[...] = (num[...] / den[...]).astype(o_ref.dtype)


def _dilated_attention(qkv):
    s = qkv.shape[0]
    for window, dil in DIL_PATTERNS:
        assert window // dil == DIL_BAND
        nb = s // dil // DIL_BAND
        assert s % (dil * DIL_BAND) == 0 and nb >= 2 and (dil * nb) % DIL_UNROLL == 0
    return pl.pallas_call(
        _dilated_kernel,
        grid=(N_HEADS,),
        in_specs=[
            pl.BlockSpec((s, HEAD_DIM), lambda h: (0, h)),
            pl.BlockSpec((s, HEAD_DIM), lambda h: (0, N_HEADS + h)),
            pl.BlockSpec((s, HEAD_DIM), lambda h: (0, 2 * N_HEADS + h)),
        ],
        out_specs=pl.BlockSpec((s, HEAD_DIM), lambda h: (0, h)),
        out_shape=jax.ShapeDtypeStruct((s, N_HEADS * HEAD_DIM), BF16),
        scratch_shapes=[pltpu.VMEM((s, HEAD_DIM), F32)] * 3 + [pltpu.VMEM((s, LANES), F32)] * 3,
        compiler_params=_params("parallel"),
        name="dilated_attention",
    )(qkv, qkv, qkv)


def _final_norm_kernel(x_ref, g_ref, o_ref):
    o_ref[...] = _rms_normalize(x_ref[...], g_ref[...])


def _final_norm(x, gain, *, tm=512):
    m, k = x.shape
    assert m % tm == 0
    return pl.pallas_call(
        _final_norm_kernel,
        grid=(m // tm,),
        in_specs=[pl.BlockSpec((tm, k), lambda i: (i, 0)), pl.BlockSpec((1, k), lambda i: (0, 0))],
        out_specs=pl.BlockSpec((tm, k), lambda i: (i, 0)),
        out_shape=jax.ShapeDtypeStruct((m, k), F32),
        compiler_params=_params("parallel"),
        name="final_norm",
    )(x, gain.reshape(1, k))


def _rope_tables(n_pos, dim):
    inv = 1.0 / (ROPE_THETA ** (jnp.arange(0, dim, 2, dtype=F32) / dim))
    ang = jnp.arange(n_pos, dtype=F32)[:, None] * inv[None, :]
    return jnp.cos(ang), jnp.sin(ang)


def _rotate_half_columns(w):
    half = w.shape[1] // 2
    return jnp.concatenate([-w[:, half:], w[:, :half]], axis=1)


def _pad_cols(w, width):
    return jnp.pad(w, ((0, 0), (0, width - w.shape[1])))


def _mla_weights(w_down, w_uq, w_ukv):
    lat = MLA_Q_RANK + MLA_KV_RANK
    w_rope = w_down[:, lat:]
    w_down_ext = jnp.concatenate(
        [w_down[:, :lat], _pad_cols(w_rope, LANES), _pad_cols(_rotate_half_columns(w_rope), LANES)],
        axis=1).astype(BF16)
    wq = w_uq.reshape(MLA_Q_RANK, MLA_HEADS, MLA_NOPE_DIM + MLA_ROPE_DIM)
    heads = []
    for h in range(MLA_HEADS):
        w_r = wq[:, h, MLA_NOPE_DIM:]
        heads += [wq[:, h, :MLA_NOPE_DIM], _pad_cols(w_r, LANES),
                  _pad_cols(_rotate_half_columns(w_r), LANES)]
    wq_ext = jnp.concatenate(heads, axis=1).astype(BF16)
    wkv = w_ukv.reshape(MLA_KV_RANK, MLA_HEADS, MLA_NOPE_DIM + MLA_V_DIM)
    wkv = jnp.concatenate([wkv[:, :, :MLA_NOPE_DIM].reshape(MLA_KV_RANK, -1),
                           wkv[:, :, MLA_NOPE_DIM:].reshape(MLA_KV_RANK, -1)], axis=1).astype(BF16)
    return w_down_ext, wq_ext, wkv


def kernel(x, attn_norm, ffn_norm, final_norm, moba_w_qkv, moba_w_o, mla_w_down, mla_q_norm,
           mla_kv_norm, mla_w_uq, mla_w_ukv, mla_w_o, dil_w_qkv, dil_w_o, ffn_w_up, ffn_conv_w,
           ffn_conv_b, ffn_w_down):
    b, s, d = x.shape
    assert b == 1 and d == D_MODEL
    cos, sin = _rope_tables(s, HEAD_DIM)
    cos_full = jnp.concatenate([cos, cos], axis=1)
    sin_signed = jnp.concatenate([-sin, sin], axis=1)
    cos_r, sin_r = _rope_tables(s, MLA_ROPE_DIM)
    cos_r = _pad_cols(jnp.concatenate([cos_r, cos_r], axis=1), LANES)
    sin_r = _pad_cols(jnp.concatenate([sin_r, sin_r], axis=1), LANES)

    qkv_w = {0: moba_w_qkv, 2: dil_w_qkv}
    out_w = {0: moba_w_o, 1: mla_w_o, 2: dil_w_o}
    down_w = ffn_w_down.astype(BF16)

    h = x.reshape(s, d)
    for i in range(DEPTH):
        j = i // N_MIXERS
        kind = i % N_MIXERS
        if kind == 1:
            w_down_ext, wq_ext, wkv = _mla_weights(mla_w_down[j], mla_w_uq[j], mla_w_ukv[j])
            cq, ckv, k_rope = _mla_down(h, attn_norm[i], w_down_ext, mla_q_norm[j], mla_kv_norm[j],
                                        cos_r, sin_r)
            q_full, kv = _mla_up(cq, ckv, wq_ext, wkv, cos_r, sin_r)
            mix = _mla_attention(q_full, kv, k_rope)
        else:
            qkv = _qkv_projection(h, attn_norm[i], qkv_w[kind], j, cos_full, sin_signed)
            mix = _moba_attention(qkv) if kind == 0 else _dilated_attention(qkv)
        h = _matmul_residual(mix, out_w[kind], j, h, tn=1024)
        act = _ffn_up(h, ffn_norm[i], ffn_w_up, ffn_conv_w, ffn_conv_b, i)
        h = _matmul_residual(act, down_w, i, h, tn=512)
    return _final_norm(h, final_norm).reshape(b, s, d)
```

```python
import functools
import math

import jax
import jax.numpy as jnp
from jax import lax
from jax.experimental import pallas as pl
from jax.experimental.pallas import tpu as pltpu

F32 = jnp.float32
BF16 = jnp.bfloat16

D_MODEL = 2048
DEPTH = 4
N_MIXERS = 3
ROPE_THETA = 10000.0
NORM_EPS = 1e-6

N_HEADS = 16
HEAD_DIM = D_MODEL // N_HEADS
MOBA_BLOCK = 256
MOBA_TOPK = 3

MLA_HEADS = 16
MLA_Q_RANK = 512
MLA_KV_RANK = 512
MLA_NOPE_DIM = 128
MLA_ROPE_DIM = 64
MLA_V_DIM = 128

DIL_PATTERNS = ((128, 1), (512, 4), (2048, 16))
DIL_BAND = 128
DIL_UNROLL = 8

D_FF = 5632
CONV_WIDTH = 3

LANES = 128
HALO_ROWS = 16
VMEM_LIMIT_BYTES = 56 * 1024 * 1024
FLASH_TILE = 512
FLASH_PAIRS = 2

LOG2E = math.log2(math.e)
NEG_INF = float("-inf")
MASKED = -1e30

_NT = (((1,), (1,)), ((), ()))


def _params(*semantics):
    return pltpu.CompilerParams(dimension_semantics=semantics,
                                vmem_limit_bytes=VMEM_LIMIT_BYTES)


def _rms_normalize(x, gain):
    ms = jnp.mean(x * x, axis=-1, keepdims=True)
    return x * lax.rsqrt(ms + NORM_EPS) * gain


def _norm_matmul_kernel(x_ref, g_ref, w_ref, cos_ref, sin_ref, o_ref, xn_ref, *,
                        q_tiles, rope_tiles, q_scale):
    j = pl.program_id(1)

    @pl.when(j == 0)
    def _():
        xn_ref[...] = _rms_normalize(x_ref[...], g_ref[...]).astype(BF16)

    acc = jnp.dot(xn_ref[...], w_ref[...], preferred_element_type=F32)

    @pl.when(j < rope_tiles)
    def _():
        col_scale = jnp.where(j < q_tiles, q_scale, 1.0)
        cos = cos_ref[...] * col_scale
        sin = sin_ref[...] * col_scale
        for c in range(acc.shape[1] // HEAD_DIM):
            xh = acc[:, c * HEAD_DIM:(c + 1) * HEAD_DIM]
            roped = xh * cos + pltpu.roll(xh, HEAD_DIM // 2, axis=1) * sin
            o_ref[:, c * HEAD_DIM:(c + 1) * HEAD_DIM] = roped.astype(o_ref.dtype)

    @pl.when(j >= rope_tiles)
    def _():
        o_ref[...] = acc.astype(o_ref.dtype)


def _qkv_projection(x, gain, w, layer, cos_full, sin_signed, *, tm=1024, tn=1024):
    m, k = x.shape
    n = w.shape[2]
    width = N_HEADS * HEAD_DIM
    assert m % tm == 0 and n % tn == 0 and width % tn == 0 and n == 3 * width
    kern = functools.partial(_norm_matmul_kernel, q_tiles=width // tn, rope_tiles=2 * width // tn,
                             q_scale=HEAD_DIM ** -0.5 * LOG2E)
    return pl.pallas_call(
        kern,
        grid=(m // tm, n // tn),
        in_specs=[
            pl.BlockSpec((tm, k), lambda i, j: (i, 0)),
            pl.BlockSpec((1, k), lambda i, j: (0, 0)),
            pl.BlockSpec((None, k, tn), lambda i, j: (layer, 0, j)),
            pl.BlockSpec((tm, HEAD_DIM), lambda i, j: (i, 0)),
            pl.BlockSpec((tm, HEAD_DIM), lambda i, j: (i, 0)),
        ],
        out_specs=pl.BlockSpec((tm, tn), lambda i, j: (i, j)),
        out_shape=jax.ShapeDtypeStruct((m, n), BF16),
        scratch_shapes=[pltpu.VMEM((tm, k), BF16)],
        compiler_params=_params("parallel", "arbitrary"),
        name="qkv_projection",
    )(x, gain.reshape(1, k), w, cos_full, sin_signed)


def _matmul_residual_kernel(a_ref, w_ref, r_ref, o_ref):
    o_ref[...] = r_ref[...] + jnp.dot(a_ref[...], w_ref[...], preferred_element_type=F32)


def _matmul_residual(a, w, layer, res, *, tm=1024, tn=512):
    m, k = a.shape
    n = w.shape[2]
    assert m % tm == 0 and n % tn == 0
    return pl.pallas_call(
        _matmul_residual_kernel,
        grid=(m // tm, n // tn),
        in_specs=[
            pl.BlockSpec((tm, k), lambda i, j: (i, 0)),
            pl.BlockSpec((None, k, tn), lambda i, j: (layer, 0, j)),
            pl.BlockSpec((tm, tn), lambda i, j: (i, j)),
        ],
        out_specs=pl.BlockSpec((tm, tn), lambda i, j: (i, j)),
        out_shape=jax.ShapeDtypeStruct((m, n), F32),
        compiler_params=_params("parallel", "arbitrary"),
        name="matmul_residual",
    )(a, w, res)


def _ffn_up_kernel(x_ref, halo_ref, g_ref, wg_ref, wv_ref, cwg_ref, cwv_ref, cbg_ref, cbv_ref,
                   o_ref, xn_ref):
    i = pl.program_id(0)
    j = pl.program_id(1)

    @pl.when(j == 0)
    def _():
        gain = g_ref[...]
        halo = _rms_normalize(halo_ref[...], gain)
        halo = jnp.where(i == 0, 0.0, halo)
        xn_ref[0:HALO_ROWS, :] = halo.astype(BF16)
        xn_ref[HALO_ROWS:, :] = _rms_normalize(x_ref[...], gain).astype(BF16)

    def conv(w_ref, cw_ref, cb_ref):
        h = jnp.dot(xn_ref[...], w_ref[...].astype(BF16), preferred_element_type=F32)
        cw = cw_ref[...]
        y = (pltpu.roll(h, 2, axis=0) * cw[0:1, :] + pltpu.roll(h, 1, axis=0) * cw[1:2, :]
             + h * cw[2:3, :])
        return y[HALO_ROWS:, :] + cb_ref[...]

    gate = conv(wg_ref, cwg_ref, cbg_ref)
    val = conv(wv_ref, cwv_ref, cbv_ref)
    o_ref[...] = (gate * (1.0 / (1.0 + jnp.exp(-gate))) * val).astype(o_ref.dtype)


def _ffn_up(x, gain, w_up, conv_w, conv_b, layer, *, tm=1024, tn=512):
    m, k = x.shape
    assert m % tm == 0 and D_FF % tn == 0 and tm % HALO_ROWS == 0
    nj = D_FF // tn
    halo_blocks = tm // HALO_ROWS
    cb = conv_b.reshape(conv_b.shape[0], 1, 2 * D_FF)
    return pl.pallas_call(
        _ffn_up_kernel,
        grid=(m // tm, nj),
        in_specs=[
            pl.BlockSpec((tm, k), lambda i, j: (i, 0)),
            pl.BlockSpec((HALO_ROWS, k), lambda i, j: (jnp.maximum(i * halo_blocks - 1, 0), 0)),
            pl.BlockSpec((1, k), lambda i, j: (0, 0)),
            pl.BlockSpec((None, k, tn), lambda i, j: (layer, 0, j)),
            pl.BlockSpec((None, k, tn), lambda i, j: (layer, 0, nj + j)),
            pl.BlockSpec((None, CONV_WIDTH, tn), lambda i, j: (layer, 0, j)),
            pl.BlockSpec((None, CONV_WIDTH, tn), lambda i, j: (layer, 0, nj + j)),
            pl.BlockSpec((None, 1, tn), lambda i, j: (layer, 0, j)),
            pl.BlockSpec((None, 1, tn), lambda i, j: (layer, 0, nj + j)),
        ],
        out_specs=pl.BlockSpec((tm, tn), lambda i, j: (i, j)),
        out_shape=jax.ShapeDtypeStruct((m, D_FF), BF16),
        scratch_shapes=[pltpu.VMEM((HALO_ROWS + tm, k), BF16)],
        compiler_params=_params("parallel", "arbitrary"),
        name="ffn_up",
    )(x, x, gain.reshape(1, k), w_up, w_up, conv_w, conv_w, cb, cb)


def _flash_kernel(qa_ref, qb_ref, ka_ref, kb_ref, v_ref, o_ref, m_ref, acc_ref,
                  s0, s1, p0, p1, a0, a1):
    t = FLASH_TILE
    sbuf, pbuf, abuf = (s0, s1), (p0, p1), (a0, a1)

    def q_tile(i, carry):
        _flash_q_tile(i, qa_ref, qb_ref, ka_ref, kb_ref, v_ref, o_ref, m_ref, acc_ref, sbuf, pbuf, abuf)
        return carry

    lax.fori_loop(0, qa_ref.shape[0] // t, q_tile, 0)


def _flash_q_tile(i, qa_ref, qb_ref, ka_ref, kb_ref, v_ref, o_ref, m_ref, acc_ref, sbuf, pbuf, abuf):
    t = FLASH_TILE
    q_rows = pl.ds(pl.multiple_of(i * t, t), t)

    def scores(j, slot):
        start = pl.multiple_of(j * t, t)
        keys = jnp.concatenate([ka_ref[pl.ds(start, t), :], kb_ref[pl.ds(start, t), :]], axis=1)
        q = jnp.concatenate([qa_ref[q_rows, :], qb_ref[q_rows, :]], axis=1)
        sbuf[slot][...] = lax.dot_general(q, keys, _NT, preferred_element_type=F32)

    def softmax(slot, causal=False):
        s = sbuf[slot][...]
        if causal:
            row = lax.broadcasted_iota(jnp.int32, (t, t), 0)
            col = lax.broadcasted_iota(jnp.int32, (t, t), 1)
            s = jnp.where(col <= row, s, NEG_INF)
        m_prev = m_ref[...]
        m_new = jnp.maximum(m_prev, jnp.max(s, axis=1, keepdims=True))
        abuf[slot][...] = jnp.exp2(m_prev - m_new)
        pbuf[slot][...] = jnp.exp2(s - jnp.tile(m_new, (1, t // LANES))).astype(BF16)
        m_ref[...] = m_new

    def accumulate(j, slot):
        start = pl.multiple_of(j * t, t)
        vals = jnp.concatenate([v_ref[pl.ds(start, t), :], jnp.ones((t, LANES), BF16)], axis=1)
        acc_ref[...] = (jnp.tile(abuf[slot][...], (1, 2)) * acc_ref[...]
                        + jnp.dot(pbuf[slot][...], vals, preferred_element_type=F32))

    m_ref[...] = jnp.full(m_ref.shape, MASKED, F32)
    acc_ref[...] = jnp.zeros(acc_ref.shape, F32)
    scores(i, 0)
    scores(0, 1)
    softmax(0, causal=True)

    def pair(k):
        scores(2 * k + 1, 0)
        accumulate(jnp.where(k == 0, i, 2 * k - 1), 0)
        softmax(1)
        scores(jnp.minimum(2 * k + 2, i - 1), 1)
        accumulate(2 * k, 1)
        softmax(0)

    pairs = i // 2

    def trip(step, carry):
        for u in range(FLASH_PAIRS):
            pair(step * FLASH_PAIRS + u)
        return carry

    lax.fori_loop(0, pairs // FLASH_PAIRS, trip, 0)

    def leftover(k, carry):
        pair(k)
        return carry

    lax.fori_loop(pairs - pairs % FLASH_PAIRS, pairs, leftover, 0)

    @pl.when(i % 2 == 0)
    def _():
        accumulate(jnp.where(i == 0, 0, i - 1), 0)

    @pl.when(i % 2 == 1)
    def _():
        accumulate(jnp.where(i == 1, i, i - 2), 0)
        softmax(1)
        accumulate(i - 1, 1)

    acc = acc_ref[...]
    o_ref[q_rows, :] = (acc[:, :LANES] / acc[:, LANES:]).astype(o_ref.dtype)


def _flash_attention(qa, qa_col, qb, qb_col, ka, ka_col, kb, kb_col, v, v_col, *, heads, name):
    s = ka.shape[0]
    t = FLASH_TILE
    assert s % t == 0

    def column(col):
        return pl.BlockSpec((s, LANES), lambda h: (0, col(h)))

    return pl.pallas_call(
        _flash_kernel,
        grid=(heads,),
        in_specs=[column(qa_col), column(qb_col), column(ka_col), column(kb_col), column(v_col)],
        out_specs=column(lambda h: h),
        out_shape=jax.ShapeDtypeStruct((s, heads * LANES), BF16),
        scratch_shapes=[
            pltpu.VMEM((t, LANES), F32),
            pltpu.VMEM((t, 2 * LANES), F32),
            pltpu.VMEM((t, t), F32), pltpu.VMEM((t, t), F32),
            pltpu.VMEM((t, t), BF16), pltpu.VMEM((t, t), BF16),
            pltpu.VMEM((t, LANES), F32), pltpu.VMEM((t, LANES), F32),
        ],
        compiler_params=_params("parallel"),
        name=name,
    )(qa, qb, ka, kb, v)


def _moba_select_kernel(q_ref, k_ref, bias_ref, kmh_ref, kml_ref, *, nb):
    i = pl.program_id(1)
    bs = MOBA_BLOCK
    tq = q_ref.shape[0]

    @pl.when(i == 0)
    def _():
        kmh_ref[...] = jnp.zeros(kmh_ref.shape, F32)
        kml_ref[...] = jnp.zeros(kml_ref.shape, F32)
        for b in range(nb):
            km = jnp.mean(k_ref[b * bs:(b + 1) * bs, :].astype(F32), axis=0, keepdims=True)
            hi = km.astype(BF16).astype(F32)
            kmh_ref[b:b + 1, :] = hi
            kml_ref[b:b + 1, :] = km - hi

    nbp = kmh_ref.shape[0]
    q = q_ref[...]
    gate = (lax.dot_general(kmh_ref[...].astype(BF16), q, _NT, preferred_element_type=F32)
            + lax.dot_general(kml_ref[...].astype(BF16), q, _NT, preferred_element_type=F32))
    blk = lax.broadcasted_iota(jnp.int32, gate.shape, 0).astype(F32)
    pos = lax.broadcasted_iota(jnp.int32, gate.shape, 1)
    own = (i * (tq // bs) + jnp.right_shift(pos, bs.bit_length() - 1)).astype(F32)
    gate = jnp.where(blk < own, gate, NEG_INF)
    sel = jnp.zeros(gate.shape, F32)
    for r in range(min(MOBA_TOPK, nb)):
        mx = jnp.max(gate, axis=0, keepdims=True)
        first = jnp.min(jnp.where(gate == mx, blk, float(nbp)), axis=0, keepdims=True)
        pick = blk == first
        sel = jnp.maximum(sel, jnp.where(pick, jnp.where(own > r, 1.0, 0.0), 0.0))
        gate = jnp.where(pick, NEG_INF, gate)
    visible = jnp.where(blk == own, 1.0, sel)
    bias = jnp.where(visible > 0.5, 0.0, MASKED)
    bias = jnp.where(blk < nb, bias, 0.0)
    padded = jnp.concatenate([bias, jnp.zeros((LANES - nbp, tq), F32)], axis=0)
    bias_ref[...] = padded.T.astype(BF16)


def _moba_attention(qkv, *, tq=1024):
    s = qkv.shape[0]
    bs = MOBA_BLOCK
    assert s % bs == 0 and s % tq == 0 and tq % bs == 0
    nb = s // bs
    nbp = -(-nb // 8) * 8
    assert nbp < LANES
    bias = pl.pallas_call(
        functools.partial(_moba_select_kernel, nb=nb),
        grid=(N_HEADS, s // tq),
        in_specs=[
            pl.BlockSpec((tq, HEAD_DIM), lambda h, i: (i, h)),
            pl.BlockSpec((s, HEAD_DIM), lambda h, i: (0, N_HEADS + h)),
        ],
        out_specs=pl.BlockSpec((tq, LANES), lambda h, i: (i, h)),
        out_shape=jax.ShapeDtypeStruct((s, N_HEADS * LANES), BF16),
        scratch_shapes=[pltpu.VMEM((nbp, HEAD_DIM), F32), pltpu.VMEM((nbp, HEAD_DIM), F32)],
        compiler_params=_params("parallel", "arbitrary"),
        name="moba_select",
    )(qkv, qkv)
    key_block = (jnp.arange(s, dtype=jnp.int32) // bs)[:, None]
    onehot = (key_block == jnp.arange(LANES, dtype=jnp.int32)[None, :]).astype(BF16)
    return _flash_attention(
        qkv, lambda h: h, bias, lambda h: h,
        qkv, lambda h: N_HEADS + h, onehot, lambda h: 0,
        qkv, lambda h: 2 * N_HEADS + h, heads=N_HEADS, name="moba_flash")


def _mla_down_kernel(x_ref, g_ref, w_ref, qn_ref, kvn_ref, cos_ref, sin_ref,
                     cq_ref, ckv_ref, kr_ref):
    xn = _rms_normalize(x_ref[...], g_ref[...]).astype(BF16)
    lat = jnp.dot(xn, w_ref[...], preferred_element_type=F32)
    q0, kv0, r0, rr0 = 0, MLA_Q_RANK, MLA_Q_RANK + MLA_KV_RANK, MLA_Q_RANK + MLA_KV_RANK + LANES
    cq_ref[...] = _rms_normalize(lat[:, q0:kv0], qn_ref[...]).astype(BF16)
    ckv_ref[...] = _rms_normalize(lat[:, kv0:r0], kvn_ref[...]).astype(BF16)
    kr_ref[...] = (lat[:, r0:rr0] * cos_ref[...] + lat[:, rr0:rr0 + LANES] * sin_ref[...]).astype(BF16)


def _mla_down(x, gain, w_ext, q_norm, kv_norm, cos_r, sin_r, *, tm=512):
    m, k = x.shape
    n = w_ext.shape[1]
    assert m % tm == 0
    row = lambda i: (i, 0)
    fixed = lambda i: (0, 0)
    return pl.pallas_call(
        _mla_down_kernel,
        grid=(m // tm,),
        in_specs=[
            pl.BlockSpec((tm, k), row),
            pl.BlockSpec((1, k), fixed),
            pl.BlockSpec((k, n), fixed),
            pl.BlockSpec((1, MLA_Q_RANK), fixed),
            pl.BlockSpec((1, MLA_KV_RANK), fixed),
            pl.BlockSpec((tm, LANES), row),
            pl.BlockSpec((tm, LANES), row),
        ],
        out_specs=[
            pl.BlockSpec((tm, MLA_Q_RANK), row),
            pl.BlockSpec((tm, MLA_KV_RANK), row),
            pl.BlockSpec((tm, LANES), row),
        ],
        out_shape=[
            jax.ShapeDtypeStruct((m, MLA_Q_RANK), BF16),
            jax.ShapeDtypeStruct((m, MLA_KV_RANK), BF16),
            jax.ShapeDtypeStruct((m, LANES), BF16),
        ],
        compiler_params=_params("parallel"),
        name="mla_down",
    )(x, gain.reshape(1, k), w_ext, q_norm.reshape(1, -1), kv_norm.reshape(1, -1), cos_r, sin_r)


def _mla_up_kernel(cq_ref, ckv_ref, wq_ref, wkv_ref, cos_ref, sin_ref, q_ref, kv_ref, *, q_scale):
    cq = cq_ref[...]
    cos = cos_ref[...] * q_scale
    sin = sin_ref[...] * q_scale
    per_head = MLA_NOPE_DIM + 2 * LANES
    for h in range(MLA_HEADS):
        r = jnp.dot(cq, wq_ref[:, h * per_head:(h + 1) * per_head], preferred_element_type=F32)
        rope = r[:, MLA_NOPE_DIM:MLA_NOPE_DIM + LANES] * cos + r[:, MLA_NOPE_DIM + LANES:] * sin
        q_ref[:, 2 * h * LANES:(2 * h + 1) * LANES] = (r[:, :MLA_NOPE_DIM] * q_scale).astype(BF16)
        q_ref[:, (2 * h + 1) * LANES:(2 * h + 2) * LANES] = rope.astype(BF16)
    kv_ref[...] = jnp.dot(ckv_ref[...], wkv_ref[...], preferred_element_type=F32).astype(BF16)


def _mla_up(cq, ckv, wq_ext, wkv, cos_r, sin_r, *, tm=512):
    m = cq.shape[0]
    assert m % tm == 0 and MLA_NOPE_DIM == LANES
    row = lambda i: (i, 0)
    fixed = lambda i: (0, 0)
    nq = MLA_HEADS * 2 * LANES
    nkv = wkv.shape[1]
    kern = functools.partial(_mla_up_kernel,
                             q_scale=(MLA_NOPE_DIM + MLA_ROPE_DIM) ** -0.5 * LOG2E)
    return pl.pallas_call(
        kern,
        grid=(m // tm,),
        in_specs=[
            pl.BlockSpec((tm, MLA_Q_RANK), row),
            pl.BlockSpec((tm, MLA_KV_RANK), row),
            pl.BlockSpec(wq_ext.shape, fixed),
            pl.BlockSpec(wkv.shape, fixed),
            pl.BlockSpec((tm, LANES), row),
            pl.BlockSpec((tm, LANES), row),
        ],
        out_specs=[pl.BlockSpec((tm, nq), row), pl.BlockSpec((tm, nkv), row)],
        out_shape=[jax.ShapeDtypeStruct((m, nq), BF16), jax.ShapeDtypeStruct((m, nkv), BF16)],
        compiler_params=_params("parallel"),
        name="mla_up",
    )(cq, ckv, wq_ext, wkv, cos_r, sin_r)


def _mla_attention(q_full, kv, k_rope):
    return _flash_attention(
        q_full, lambda h: 2 * h, q_full, lambda h: 2 * h + 1,
        kv, lambda h: h, k_rope, lambda h: 0,
        kv, lambda h: MLA_HEADS + h, heads=MLA_HEADS, name="mla_flash")


def _dilated_kernel(q_ref, k_ref, v_ref, o_ref, qf, kf, vf, num, den, mx):
    band = DIL_BAND
    s_len = q_ref.shape[0]
    qf[...] = q_ref[...].astype(F32)
    kf[...] = k_ref[...].astype(F32)
    vf[...] = v_ref[...].astype(F32)

    row = lax.broadcasted_iota(jnp.int32, (band, 2 * band), 0)
    col = lax.broadcasted_iota(jnp.int32, (band, 2 * band), 1)
    outside = jnp.where(col < band, row - col, col - band - row)
    bias_inner = jnp.where(outside <= 0, 0.0, NEG_INF)
    bias_first = jnp.where(col <= row, 0.0, NEG_INF)
    ones = jnp.ones((2 * band, LANES), BF16)

    def rows(start, n, dil):
        return pl.ds(start, n) if dil == 1 else pl.ds(start, n, stride=dil)

    def block(dil, r, n, bias, first_pattern):
        q_start = n * band * dil + r
        k_start = jnp.maximum(n - 1, 0) * band * dil + r
        q = qf[rows(q_start, band, dil), :].astype(BF16)
        k = kf[rows(k_start, 2 * band, dil), :].astype(BF16)
        v = vf[rows(k_start, 2 * band, dil), :].astype(BF16)
        s = lax.dot_general(q, k, _NT, preferred_element_type=F32) + bias
        m = jnp.max(s, axis=1, keepdims=True)
        p = jnp.exp2(s - m).astype(BF16)
        pv = jnp.dot(p, jnp.concatenate([v, ones], axis=1), preferred_element_type=F32)
        m_blk = jnp.broadcast_to(m, (band, LANES))
        dst = rows(q_start, band, dil)
        if first_pattern:
            num[dst, :] = pv[:, :LANES]
            den[dst, :] = pv[:, LANES:]
            mx[dst, :] = m_blk
        else:
            m_old = mx[dst, :]
            m_new = jnp.maximum(m_old, m_blk)
            a_old = jnp.exp2(m_old - m_new)
            a_blk = jnp.exp2(m_blk - m_new)
            num[dst, :] = a_old * num[dst, :] + a_blk * pv[:, :LANES]
            den[dst, :] = a_old * den[dst, :] + a_blk * pv[:, LANES:]
            mx[dst, :] = m_new

    for idx, (_, dil) in enumerate(DIL_PATTERNS):
        nb = s_len // dil // band
        aligned = math.gcd(nb, DIL_UNROLL)

        def step(t, carry, dil=dil, nb=nb, aligned=aligned, first=(idx == 0)):
            for u in range(DIL_UNROLL):
                flat = t * DIL_UNROLL + u
                r, n = flat // nb, flat % nb
                bias = jnp.where(n == 0, bias_first, bias_inner) if u % aligned == 0 else bias_inner
                block(dil, r, n, bias, first)
            return carry

        lax.fori_loop(0, dil * nb // DIL_UNROLL, step, 0)

    o_ref[...] = (num[...] / den[...]).astype(o_ref.dtype)


def _dilated_attention(qkv):
    s = qkv.shape[0]
    for window, dil in DIL_PATTERNS:
        assert window // dil == DIL_BAND
        nb = s // dil // DIL_BAND
        assert s % (dil * DIL_BAND) == 0 and nb >= 2 and (dil * nb) % DIL_UNROLL == 0
    return pl.pallas_call(
        _dilated_kernel,
        grid=(N_HEADS,),
        in_specs=[
            pl.BlockSpec((s, HEAD_DIM), lambda h: (0, h)),
            pl.BlockSpec((s, HEAD_DIM), lambda h: (0, N_HEADS + h)),
            pl.BlockSpec((s, HEAD_DIM), lambda h: (0, 2 * N_HEADS + h)),
        ],
        out_specs=pl.BlockSpec((s, HEAD_DIM), lambda h: (0, h)),
        out_shape=jax.ShapeDtypeStruct((s, N_HEADS * HEAD_DIM), BF16),
        scratch_shapes=[pltpu.VMEM((s, HEAD_DIM), F32)] * 3 + [pltpu.VMEM((s, LANES), F32)] * 3,
        compiler_params=_params("parallel"),
        name="dilated_attention",
    )(qkv, qkv, qkv)


def _final_norm_kernel(x_ref, g_ref, o_ref):
    o_ref[...] = _rms_normalize(x_ref[...], g_ref[...])


def _final_norm(x, gain, *, tm=512):
    m, k = x.shape
    assert m % tm == 0
    return pl.pallas_call(
        _final_norm_kernel,
        grid=(m // tm,),
        in_specs=[pl.BlockSpec((tm, k), lambda i: (i, 0)), pl.BlockSpec((1, k), lambda i: (0, 0))],
        out_specs=pl.BlockSpec((tm, k), lambda i: (i, 0)),
        out_shape=jax.ShapeDtypeStruct((m, k), F32),
        compiler_params=_params("parallel"),
        name="final_norm",
    )(x, gain.reshape(1, k))


def _rope_tables(n_pos, dim):
    inv = 1.0 / (ROPE_THETA ** (jnp.arange(0, dim, 2, dtype=F32) / dim))
    ang = jnp.arange(n_pos, dtype=F32)[:, None] * inv[None, :]
    return jnp.cos(ang), jnp.sin(ang)


def _rotate_half_columns(w):
    half = w.shape[1] // 2
    return jnp.concatenate([-w[:, half:], w[:, :half]], axis=1)


def _pad_cols(w, width):
    return jnp.pad(w, ((0, 0), (0, width - w.shape[1])))


def _mla_weights(w_down, w_uq, w_ukv):
    lat = MLA_Q_RANK + MLA_KV_RANK
    w_rope = w_down[:, lat:]
    w_down_ext = jnp.concatenate(
        [w_down[:, :lat], _pad_cols(w_rope, LANES), _pad_cols(_rotate_half_columns(w_rope), LANES)],
        axis=1).astype(BF16)
    wq = w_uq.reshape(MLA_Q_RANK, MLA_HEADS, MLA_NOPE_DIM + MLA_ROPE_DIM)
    heads = []
    for h in range(MLA_HEADS):
        w_r = wq[:, h, MLA_NOPE_DIM:]
        heads += [wq[:, h, :MLA_NOPE_DIM], _pad_cols(w_r, LANES),
                  _pad_cols(_rotate_half_columns(w_r), LANES)]
    wq_ext = jnp.concatenate(heads, axis=1).astype(BF16)
    wkv = w_ukv.reshape(MLA_KV_RANK, MLA_HEADS, MLA_NOPE_DIM + MLA_V_DIM)
    wkv = jnp.concatenate([wkv[:, :, :MLA_NOPE_DIM].reshape(MLA_KV_RANK, -1),
                           wkv[:, :, MLA_NOPE_DIM:].reshape(MLA_KV_RANK, -1)], axis=1).astype(BF16)
    return w_down_ext, wq_ext, wkv


def kernel(x, attn_norm, ffn_norm, final_norm, moba_w_qkv, moba_w_o, mla_w_down, mla_q_norm,
           mla_kv_norm, mla_w_uq, mla_w_ukv, mla_w_o, dil_w_qkv, dil_w_o, ffn_w_up, ffn_conv_w,
           ffn_conv_b, ffn_w_down):
    b, s, d = x.shape
    assert b == 1 and d == D_MODEL
    cos, sin = _rope_tables(s, HEAD_DIM)
    cos_full = jnp.concatenate([cos, cos], axis=1)
    sin_signed = jnp.concatenate([-sin, sin], axis=1)
    cos_r, sin_r = _rope_tables(s, MLA_ROPE_DIM)
    cos_r = _pad_cols(jnp.concatenate([cos_r, cos_r], axis=1), LANES)
    sin_r = _pad_cols(jnp.concatenate([sin_r, sin_r], axis=1), LANES)

    qkv_w = {0: moba_w_qkv.astype(BF16), 2: dil_w_qkv.astype(BF16)}
    out_w = {0: moba_w_o.astype(BF16), 1: mla_w_o.astype(BF16), 2: dil_w_o.astype(BF16)}
    down_w = ffn_w_down.astype(BF16)

    h = x.reshape(s, d)
    for i in range(DEPTH):
        j = i // N_MIXERS
        kind = i % N_MIXERS
        if kind == 1:
            w_down_ext, wq_ext, wkv = _mla_weights(mla_w_down[j], mla_w_uq[j], mla_w_ukv[j])
            cq, ckv, k_rope = _mla_down(h, attn_norm[i], w_down_ext, mla_q_norm[j], mla_kv_norm[j],
                                        cos_r, sin_r)
            q_full, kv = _mla_up(cq, ckv, wq_ext, wkv, cos_r, sin_r)
            mix = _mla_attention(q_full, kv, k_rope)
        else:
            qkv = _qkv_projection(h, attn_norm[i], qkv_w[kind], j, cos_full, sin_signed)
            mix = _moba_attention(qkv) if kind == 0 else _dilated_attention(qkv)
        h = _matmul_residual(mix, out_w[kind], j, h, tn=1024)
        act = _ffn_up(h, ffn_norm[i], ffn_w_up, ffn_conv_w, ffn_conv_b, i)
        h = _matmul_residual(act, down_w, i, h, tn=512)
    return _final_norm(h, final_norm).reshape(b, s, d)
```

```python
import functools
import math

import jax
import jax.numpy as jnp
from jax import lax
from jax.experimental import pallas as pl
from jax.experimental.pallas import tpu as pltpu

F32 = jnp.float32
BF16 = jnp.bfloat16

D_MODEL = 2048
DEPTH = 4
N_MIXERS = 3
ROPE_THETA = 10000.0
NORM_EPS = 1e-6

N_HEADS = 16
HEAD_DIM = D_MODEL // N_HEADS
MOBA_BLOCK = 256
MOBA_TOPK = 3

MLA_HEADS = 16
MLA_Q_RANK = 512
MLA_KV_RANK = 512
MLA_NOPE_DIM = 128
MLA_ROPE_DIM = 64
MLA_V_DIM = 128

DIL_PATTERNS = ((128, 1), (512, 4), (2048, 16))
DIL_BAND = 128
DIL_UNROLL = 8

D_FF = 5632
CONV_WIDTH = 3

LANES = 128
HALO_ROWS = 16
VMEM_LIMIT_BYTES = 56 * 1024 * 1024
FLASH_TILE = 512
FLASH_TRIP_PAIRS = (4, 2, 1)

LOG2E = math.log2(math.e)
NEG_INF = float("-inf")
MASKED = -1e30

_NT = (((1,), (1,)), ((), ()))


def _params(*semantics):
    return pltpu.CompilerParams(dimension_semantics=semantics,
                                vmem_limit_bytes=VMEM_LIMIT_BYTES)


def _rms_normalize(x, gain):
    ms = jnp.mean(x * x, axis=-1, keepdims=True)
    return x * lax.rsqrt(ms + NORM_EPS) * gain


def _norm_matmul_kernel(x_ref, g_ref, w_ref, cos_ref, sin_ref, o_ref, xn_ref, *,
                        q_tiles, rope_tiles, q_scale):
    j = pl.program_id(1)

    @pl.when(j == 0)
    def _():
        xn_ref[...] = _rms_normalize(x_ref[...], g_ref[...]).astype(BF16)

    acc = jnp.dot(xn_ref[...], w_ref[...], preferred_element_type=F32)

    @pl.when(j < rope_tiles)
    def _():
        col_scale = jnp.where(j < q_tiles, q_scale, 1.0)
        cos = cos_ref[...] * col_scale
        sin = sin_ref[...] * col_scale
        for c in range(acc.shape[1] // HEAD_DIM):
            xh = acc[:, c * HEAD_DIM:(c + 1) * HEAD_DIM]
            roped = xh * cos + pltpu.roll(xh, HEAD_DIM // 2, axis=1) * sin
            o_ref[:, c * HEAD_DIM:(c + 1) * HEAD_DIM] = roped.astype(o_ref.dtype)

    @pl.when(j >= rope_tiles)
    def _():
        o_ref[...] = acc.astype(o_ref.dtype)


def _qkv_projection(x, gain, w, layer, cos_full, sin_signed, *, tm=1024, tn=1024):
    m, k = x.shape
    n = w.shape[2]
    width = N_HEADS * HEAD_DIM
    assert m % tm == 0 and n % tn == 0 and width % tn == 0 and n == 3 * width
    kern = functools.partial(_norm_matmul_kernel, q_tiles=width // tn, rope_tiles=2 * width // tn,
                             q_scale=HEAD_DIM ** -0.5 * LOG2E)
    return pl.pallas_call(
        kern,
        grid=(m // tm, n // tn),
        in_specs=[
            pl.BlockSpec((tm, k), lambda i, j: (i, 0)),
            pl.BlockSpec((1, k), lambda i, j: (0, 0)),
            pl.BlockSpec((None, k, tn), lambda i, j: (layer, 0, j)),
            pl.BlockSpec((tm, HEAD_DIM), lambda i, j: (i, 0)),
            pl.BlockSpec((tm, HEAD_DIM), lambda i, j: (i, 0)),
        ],
        out_specs=pl.BlockSpec((tm, tn), lambda i, j: (i, j)),
        out_shape=jax.ShapeDtypeStruct((m, n), BF16),
        scratch_shapes=[pltpu.VMEM((tm, k), BF16)],
        compiler_params=_params("parallel", "arbitrary"),
        name="qkv_projection",
    )(x, gain.reshape(1, k), w, cos_full, sin_signed)


def _matmul_residual_kernel(a_ref, w_ref, r_ref, o_ref):
    o_ref[...] = r_ref[...] + jnp.dot(a_ref[...], w_ref[...], preferred_element_type=F32)


def _matmul_residual(a, w, layer, res, *, tm=1024, tn=512):
    m, k = a.shape
    n = w.shape[2]
    assert m % tm == 0 and n % tn == 0
    return pl.pallas_call(
        _matmul_residual_kernel,
        grid=(m // tm, n // tn),
        in_specs=[
            pl.BlockSpec((tm, k), lambda i, j: (i, 0)),
            pl.BlockSpec((None, k, tn), lambda i, j: (layer, 0, j)),
            pl.BlockSpec((tm, tn), lambda i, j: (i, j)),
        ],
        out_specs=pl.BlockSpec((tm, tn), lambda i, j: (i, j)),
        out_shape=jax.ShapeDtypeStruct((m, n), F32),
        compiler_params=_params("parallel", "arbitrary"),
        name="matmul_residual",
    )(a, w, res)


def _ffn_up_kernel(x_ref, halo_ref, g_ref, wg_ref, wv_ref, cwg_ref, cwv_ref, cbg_ref, cbv_ref,
                   o_ref, xn_ref):
    i = pl.program_id(0)
    j = pl.program_id(1)

    @pl.when(j == 0)
    def _():
        gain = g_ref[...]
        halo = _rms_normalize(halo_ref[...], gain)
        halo = jnp.where(i == 0, 0.0, halo)
        xn_ref[0:HALO_ROWS, :] = halo.astype(BF16)
        xn_ref[HALO_ROWS:, :] = _rms_normalize(x_ref[...], gain).astype(BF16)

    def conv(w_ref, cw_ref, cb_ref):
        h = jnp.dot(xn_ref[...], w_ref[...].astype(BF16), preferred_element_type=F32)
        cw = cw_ref[...]
        y = (pltpu.roll(h, 2, axis=0) * cw[0:1, :] + pltpu.roll(h, 1, axis=0) * cw[1:2, :]
             + h * cw[2:3, :])
        return y[HALO_ROWS:, :] + cb_ref[...]

    gate = conv(wg_ref, cwg_ref, cbg_ref)
    val = conv(wv_ref, cwv_ref, cbv_ref)
    o_ref[...] = (gate * (1.0 / (1.0 + jnp.exp(-gate))) * val).astype(o_ref.dtype)


def _ffn_up(x, gain, w_up, conv_w, conv_b, layer, *, tm=1024, tn=512):
    m, k = x.shape
    assert m % tm == 0 and D_FF % tn == 0 and tm % HALO_ROWS == 0
    nj = D_FF // tn
    halo_blocks = tm // HALO_ROWS
    cb = conv_b.reshape(conv_b.shape[0], 1, 2 * D_FF)
    return pl.pallas_call(
        _ffn_up_kernel,
        grid=(m // tm, nj),
        in_specs=[
            pl.BlockSpec((tm, k), lambda i, j: (i, 0)),
            pl.BlockSpec((HALO_ROWS, k), lambda i, j: (jnp.maximum(i * halo_blocks - 1, 0), 0)),
            pl.BlockSpec((1, k), lambda i, j: (0, 0)),
            pl.BlockSpec((None, k, tn), lambda i, j: (layer, 0, j)),
            pl.BlockSpec((None, k, tn), lambda i, j: (layer, 0, nj + j)),
            pl.BlockSpec((None, CONV_WIDTH, tn), lambda i, j: (layer, 0, j)),
            pl.BlockSpec((None, CONV_WIDTH, tn), lambda i, j: (layer, 0, nj + j)),
            pl.BlockSpec((None, 1, tn), lambda i, j: (layer, 0, j)),
            pl.BlockSpec((None, 1, tn), lambda i, j: (layer, 0, nj + j)),
        ],
        out_specs=pl.BlockSpec((tm, tn), lambda i, j: (i, j)),
        out_shape=jax.ShapeDtypeStruct((m, D_FF), BF16),
        scratch_shapes=[pltpu.VMEM((HALO_ROWS + tm, k), BF16)],
        compiler_params=_params("parallel", "arbitrary"),
        name="ffn_up",
    )(x, x, gain.reshape(1, k), w_up, w_up, conv_w, conv_w, cb, cb)


def _flash_kernel(qa_ref, qb_ref, ka_ref, kb_ref, v_ref, o_ref, m_ref, acc_ref,
                  s0, s1, p0, p1, a0, a1):
    t = FLASH_TILE
    sbuf, pbuf, abuf = (s0, s1), (p0, p1), (a0, a1)

    def q_tile(i, carry):
        _flash_q_tile(i, qa_ref, qb_ref, ka_ref, kb_ref, v_ref, o_ref, m_ref, acc_ref, sbuf, pbuf, abuf)
        return carry

    lax.fori_loop(0, qa_ref.shape[0] // t, q_tile, 0)


def _flash_q_tile(i, qa_ref, qb_ref, ka_ref, kb_ref, v_ref, o_ref, m_ref, acc_ref, sbuf, pbuf, abuf):
    t = FLASH_TILE
    q_rows = pl.ds(pl.multiple_of(i * t, t), t)

    def scores(j, slot):
        start = pl.multiple_of(j * t, t)
        keys = jnp.concatenate([ka_ref[pl.ds(start, t), :], kb_ref[pl.ds(start, t), :]], axis=1)
        q = jnp.concatenate([qa_ref[q_rows, :], qb_ref[q_rows, :]], axis=1)
        sbuf[slot][...] = lax.dot_general(q, keys, _NT, preferred_element_type=F32)

    def softmax(slot, causal=False):
        s = sbuf[slot][...]
        if causal:
            row = lax.broadcasted_iota(jnp.int32, (t, t), 0)
            col = lax.broadcasted_iota(jnp.int32, (t, t), 1)
            s = jnp.where(col <= row, s, NEG_INF)
        m_prev = m_ref[...]
        m_new = jnp.maximum(m_prev, jnp.max(s, axis=1, keepdims=True))
        abuf[slot][...] = jnp.exp2(m_prev - m_new)
        pbuf[slot][...] = jnp.exp2(s - jnp.tile(m_new, (1, t // LANES))).astype(BF16)
        m_ref[...] = m_new

    def accumulate(j, slot):
        start = pl.multiple_of(j * t, t)
        vals = jnp.concatenate([v_ref[pl.ds(start, t), :], jnp.ones((t, LANES), BF16)], axis=1)
        acc_ref[...] = (jnp.tile(abuf[slot][...], (1, 2)) * acc_ref[...]
                        + jnp.dot(pbuf[slot][...], vals, preferred_element_type=F32))

    m_ref[...] = jnp.full(m_ref.shape, MASKED, F32)
    acc_ref[...] = jnp.zeros(acc_ref.shape, F32)
    scores(i, 0)
    scores(0, 1)
    softmax(0, causal=True)

    def pair(k):
        scores(2 * k + 1, 0)
        accumulate(jnp.where(k == 0, i, 2 * k - 1), 0)
        softmax(1)
        scores(jnp.minimum(2 * k + 2, i - 1), 1)
        accumulate(2 * k, 1)
        softmax(0)

    pairs = i // 2
    done = 0
    for size in FLASH_TRIP_PAIRS:
        trips = (pairs - done) // size

        def trip(step, carry, size=size, first=done):
            for u in range(size):
                pair(first + step * size + u)
            return carry

        lax.fori_loop(0, trips, trip, 0)
        done = done + trips * size

    @pl.when(i % 2 == 0)
    def _():
        accumulate(jnp.where(i == 0, 0, i - 1), 0)

    @pl.when(i % 2 == 1)
    def _():
        accumulate(jnp.where(i == 1, i, i - 2), 0)
        softmax(1)
        accumulate(i - 1, 1)

    acc = acc_ref[...]
    o_ref[q_rows, :] = (acc[:, :LANES] / acc[:, LANES:]).astype(o_ref.dtype)


def _flash_attention(qa, qa_col, qb, qb_col, ka, ka_col, kb, kb_col, v, v_col, *, heads, name):
    s = ka.shape[0]
    t = FLASH_TILE
    assert s % t == 0

    def column(col):
        return pl.BlockSpec((s, LANES), lambda h: (0, col(h)))

    return pl.pallas_call(
        _flash_kernel,
        grid=(heads,),
        in_specs=[column(qa_col), column(qb_col), column(ka_col), column(kb_col), column(v_col)],
        out_specs=column(lambda h: h),
        out_shape=jax.ShapeDtypeStruct((s, heads * LANES), BF16),
        scratch_shapes=[
            pltpu.VMEM((t, LANES), F32),
            pltpu.VMEM((t, 2 * LANES), F32),
            pltpu.VMEM((t, t), F32), pltpu.VMEM((t, t), F32),
            pltpu.VMEM((t, t), BF16), pltpu.VMEM((t, t), BF16),
            pltpu.VMEM((t, LANES), F32), pltpu.VMEM((t, LANES), F32),
        ],
        compiler_params=_params("parallel"),
        name=name,
    )(qa, qb, ka, kb, v)


def _moba_select_kernel(q_ref, k_ref, bias_ref, kmh_ref, kml_ref, *, nb):
    i = pl.program_id(1)
    bs = MOBA_BLOCK
    tq = q_ref.shape[0]

    @pl.when(i == 0)
    def _():
        kmh_ref[...] = jnp.zeros(kmh_ref.shape, F32)
        kml_ref[...] = jnp.zeros(kml_ref.shape, F32)
        for b in range(nb):
            km = jnp.mean(k_ref[b * bs:(b + 1) * bs, :].astype(F32), axis=0, keepdims=True)
            hi = km.astype(BF16).astype(F32)
            kmh_ref[b:b + 1, :] = hi
            kml_ref[b:b + 1, :] = km - hi

    nbp = kmh_ref.shape[0]
    q = q_ref[...]
    gate = (lax.dot_general(kmh_ref[...].astype(BF16), q, _NT, preferred_element_type=F32)
            + lax.dot_general(kml_ref[...].astype(BF16), q, _NT, preferred_element_type=F32))
    blk = lax.broadcasted_iota(jnp.int32, gate.shape, 0).astype(F32)
    pos = lax.broadcasted_iota(jnp.int32, gate.shape, 1)
    own = (i * (tq // bs) + jnp.right_shift(pos, bs.bit_length() - 1)).astype(F32)
    gate = jnp.where(blk < own, gate, NEG_INF)
    sel = jnp.zeros(gate.shape, F32)
    for r in range(min(MOBA_TOPK, nb)):
        mx = jnp.max(gate, axis=0, keepdims=True)
        first = jnp.min(jnp.where(gate == mx, blk, float(nbp)), axis=0, keepdims=True)
        pick = blk == first
        sel = jnp.maximum(sel, jnp.where(pick, jnp.where(own > r, 1.0, 0.0), 0.0))
        gate = jnp.where(pick, NEG_INF, gate)
    visible = jnp.where(blk == own, 1.0, sel)
    bias = jnp.where(visible > 0.5, 0.0, MASKED)
    bias = jnp.where(blk < nb, bias, 0.0)
    padded = jnp.concatenate([bias, jnp.zeros((LANES - nbp, tq), F32)], axis=0)
    bias_ref[...] = padded.T.astype(BF16)


def _moba_attention(qkv, *, tq=1024):
    s = qkv.shape[0]
    bs = MOBA_BLOCK
    assert s % bs == 0 and s % tq == 0 and tq % bs == 0
    nb = s // bs
    nbp = -(-nb // 8) * 8
    assert nbp < LANES
    bias = pl.pallas_call(
        functools.partial(_moba_select_kernel, nb=nb),
        grid=(N_HEADS, s // tq),
        in_specs=[
            pl.BlockSpec((tq, HEAD_DIM), lambda h, i: (i, h)),
            pl.BlockSpec((s, HEAD_DIM), lambda h, i: (0, N_HEADS + h)),
        ],
        out_specs=pl.BlockSpec((tq, LANES), lambda h, i: (i, h)),
        out_shape=jax.ShapeDtypeStruct((s, N_HEADS * LANES), BF16),
        scratch_shapes=[pltpu.VMEM((nbp, HEAD_DIM), F32), pltpu.VMEM((nbp, HEAD_DIM), F32)],
        compiler_params=_params("parallel", "arbitrary"),
        name="moba_select",
    )(qkv, qkv)
    key_block = (jnp.arange(s, dtype=jnp.int32) // bs)[:, None]
    onehot = (key_block == jnp.arange(LANES, dtype=jnp.int32)[None, :]).astype(BF16)
    return _flash_attention(
        qkv, lambda h: h, bias, lambda h: h,
        qkv, lambda h: N_HEADS + h, onehot, lambda h: 0,
        qkv, lambda h: 2 * N_HEADS + h, heads=N_HEADS, name="moba_flash")


def _mla_down_kernel(x_ref, g_ref, w_ref, qn_ref, kvn_ref, cos_ref, sin_ref,
                     cq_ref, ckv_ref, kr_ref):
    xn = _rms_normalize(x_ref[...], g_ref[...]).astype(BF16)
    lat = jnp.dot(xn, w_ref[...], preferred_element_type=F32)
    q0, kv0, r0, rr0 = 0, MLA_Q_RANK, MLA_Q_RANK + MLA_KV_RANK, MLA_Q_RANK + MLA_KV_RANK + LANES
    cq_ref[...] = _rms_normalize(lat[:, q0:kv0], qn_ref[...]).astype(BF16)
    ckv_ref[...] = _rms_normalize(lat[:, kv0:r0], kvn_ref[...]).astype(BF16)
    kr_ref[...] = (lat[:, r0:rr0] * cos_ref[...] + lat[:, rr0:rr0 + LANES] * sin_ref[...]).astype(BF16)


def _mla_down(x, gain, w_ext, q_norm, kv_norm, cos_r, sin_r, *, tm=512):
    m, k = x.shape
    n = w_ext.shape[1]
    assert m % tm == 0
    row = lambda i: (i, 0)
    fixed = lambda i: (0, 0)
    return pl.pallas_call(
        _mla_down_kernel,
        grid=(m // tm,),
        in_specs=[
            pl.BlockSpec((tm, k), row),
            pl.BlockSpec((1, k), fixed),
            pl.BlockSpec((k, n), fixed),
            pl.BlockSpec((1, MLA_Q_RANK), fixed),
            pl.BlockSpec((1, MLA_KV_RANK), fixed),
            pl.BlockSpec((tm, LANES), row),
            pl.BlockSpec((tm, LANES), row),
        ],
        out_specs=[
            pl.BlockSpec((tm, MLA_Q_RANK), row),
            pl.BlockSpec((tm, MLA_KV_RANK), row),
            pl.BlockSpec((tm, LANES), row),
        ],
        out_shape=[
            jax.ShapeDtypeStruct((m, MLA_Q_RANK), BF16),
            jax.ShapeDtypeStruct((m, MLA_KV_RANK), BF16),
            jax.ShapeDtypeStruct((m, LANES), BF16),
        ],
        compiler_params=_params("parallel"),
        name="mla_down",
    )(x, gain.reshape(1, k), w_ext, q_norm.reshape(1, -1), kv_norm.reshape(1, -1), cos_r, sin_r)


def _mla_up_kernel(cq_ref, ckv_ref, wq_ref, wkv_ref, cos_ref, sin_ref, q_ref, kv_ref, *, q_scale):
    cq = cq_ref[...]
    cos = cos_ref[...] * q_scale
    sin = sin_ref[...] * q_scale
    per_head = MLA_NOPE_DIM + 2 * LANES
    for h in range(MLA_HEADS):
        r = jnp.dot(cq, wq_ref[:, h * per_head:(h + 1) * per_head], preferred_element_type=F32)
        rope = r[:, MLA_NOPE_DIM:MLA_NOPE_DIM + LANES] * cos + r[:, MLA_NOPE_DIM + LANES:] * sin
        q_ref[:, 2 * h * LANES:(2 * h + 1) * LANES] = (r[:, :MLA_NOPE_DIM] * q_scale).astype(BF16)
        q_ref[:, (2 * h + 1) * LANES:(2 * h + 2) * LANES] = rope.astype(BF16)
    kv_ref[...] = jnp.dot(ckv_ref[...], wkv_ref[...], preferred_element_type=F32).astype(BF16)


def _mla_up(cq, ckv, wq_ext, wkv, cos_r, sin_r, *, tm=512):
    m = cq.shape[0]
    assert m % tm == 0 and MLA_NOPE_DIM == LANES
    row = lambda i: (i, 0)
    fixed = lambda i: (0, 0)
    nq = MLA_HEADS * 2 * LANES
    nkv = wkv.shape[1]
    kern = functools.partial(_mla_up_kernel,
                             q_scale=(MLA_NOPE_DIM + MLA_ROPE_DIM) ** -0.5 * LOG2E)
    return pl.pallas_call(
        kern,
        grid=(m // tm,),
        in_specs=[
            pl.BlockSpec((tm, MLA_Q_RANK), row),
            pl.BlockSpec((tm, MLA_KV_RANK), row),
            pl.BlockSpec(wq_ext.shape, fixed),
            pl.BlockSpec(wkv.shape, fixed),
            pl.BlockSpec((tm, LANES), row),
            pl.BlockSpec((tm, LANES), row),
        ],
        out_specs=[pl.BlockSpec((tm, nq), row), pl.BlockSpec((tm, nkv), row)],
        out_shape=[jax.ShapeDtypeStruct((m, nq), BF16), jax.ShapeDtypeStruct((m, nkv), BF16)],
        compiler_params=_params("parallel"),
        name="mla_up",
    )(cq, ckv, wq_ext, wkv, cos_r, sin_r)


def _mla_attention(q_full, kv, k_rope):
    return _flash_attention(
        q_full, lambda h: 2 * h, q_full, lambda h: 2 * h + 1,
        kv, lambda h: h, k_rope, lambda h: 0,
        kv, lambda h: MLA_HEADS + h, heads=MLA_HEADS, name="mla_flash")


def _dilated_kernel(q_ref, k_ref, v_ref, o_ref, qf, kf, vf, num, den, mx):
    band = DIL_BAND
    s_len = q_ref.shape[0]
    qf[...] = q_ref[...].astype(F32)
    kf[...] = k_ref[...].astype(F32)
    vf[...] = v_ref[...].astype(F32)

    row = lax.broadcasted_iota(jnp.int32, (band, 2 * band), 0)
    col = lax.broadcasted_iota(jnp.int32, (band, 2 * band), 1)
    outside = jnp.where(col < band, row - col, col - band - row)
    bias_inner = jnp.where(outside <= 0, 0.0, NEG_INF)
    bias_first = jnp.where(col <= row, 0.0, NEG_INF)
    ones = jnp.ones((2 * band, LANES), BF16)

    def rows(start, n, dil):
        return pl.ds(start, n) if dil == 1 else pl.ds(start, n, stride=dil)

    def block(dil, r, n, bias, first_pattern):
        q_start = n * band * dil + r
        k_start = jnp.maximum(n - 1, 0) * band * dil + r
        q = qf[rows(q_start, band, dil), :].astype(BF16)
        k = kf[rows(k_start, 2 * band, dil), :].astype(BF16)
        v = vf[rows(k_start, 2 * band, dil), :].astype(BF16)
        s = lax.dot_general(q, k, _NT, preferred_element_type=F32) + bias
        m = jnp.max(s, axis=1, keepdims=True)
        p = jnp.exp2(s - m).astype(BF16)
        pv = jnp.dot(p, jnp.concatenate([v, ones], axis=1), preferred_element_type=F32)
        m_blk = jnp.broadcast_to(m, (band, LANES))
        dst = rows(q_start, band, dil)
        if first_pattern:
            num[dst, :] = pv[:, :LANES]
            den[dst, :] = pv[:, LANES:]
            mx[dst, :] = m_blk
        else:
            m_old = mx[dst, :]
            m_new = jnp.maximum(m_old, m_blk)
            a_old = jnp.exp2(m_old - m_new)
            a_blk = jnp.exp2(m_blk - m_new)
            num[dst, :] = a_old * num[dst, :] + a_blk * pv[:, :LANES]
            den[dst, :] = a_old * den[dst, :] + a_blk * pv[:, LANES:]
            mx[dst, :] = m_new

    for idx, (_, dil) in enumerate(DIL_PATTERNS):
        nb = s_len // dil // band
        aligned = math.gcd(nb, DIL_UNROLL)

        def step(t, carry, dil=dil, nb=nb, aligned=aligned, first=(idx == 0)):
            for u in range(DIL_UNROLL):
                flat = t * DIL_UNROLL + u
                r, n = flat // nb, flat % nb
                bias = jnp.where(n == 0, bias_first, bias_inner) if u % aligned == 0 else bias_inner
                block(dil, r, n, bias, first)
            return carry

        lax.fori_loop(0, dil * nb // DIL_UNROLL, step, 0)

    o_ref[...] = (num[...] / den[...]).astype(o_ref.dtype)


def _dilated_attention(qkv):
    s = qkv.shape[0]
    for window, dil in DIL_PATTERNS:
        assert window // dil == DIL_BAND
        nb = s // dil // DIL_BAND
        assert s % (dil * DIL_BAND) == 0 and nb >= 2 and (dil * nb) % DIL_UNROLL == 0
    return pl.pallas_call(
        _dilated_kernel,
        grid=(N_HEADS,),
        in_specs=[
            pl.BlockSpec((s, HEAD_DIM), lambda h: (0, h)),
            pl.BlockSpec((s, HEAD_DIM), lambda h: (0, N_HEADS + h)),
            pl.BlockSpec((s, HEAD_DIM), lambda h: (0, 2 * N_HEADS + h)),
        ],
        out_specs=pl.BlockSpec((s, HEAD_DIM), lambda h: (0, h)),
        out_shape=jax.ShapeDtypeStruct((s, N_HEADS * HEAD_DIM), BF16),
        scratch_shapes=[pltpu.VMEM((s, HEAD_DIM), F32)] * 3 + [pltpu.VMEM((s, LANES), F32)] * 3,
        compiler_params=_params("parallel"),
        name="dilated_attention",
    )(qkv, qkv, qkv)


def _final_norm_kernel(x_ref, g_ref, o_ref):
    o_ref[...] = _rms_normalize(x_ref[...], g_ref[...])


def _final_norm(x, gain, *, tm=512):
    m, k = x.shape
    assert m % tm == 0
    return pl.pallas_call(
        _final_norm_kernel,
        grid=(m // tm,),
        in_specs=[pl.BlockSpec((tm, k), lambda i: (i, 0)), pl.BlockSpec((1, k), lambda i: (0, 0))],
        out_specs=pl.BlockSpec((tm, k), lambda i: (i, 0)),
        out_shape=jax.ShapeDtypeStruct((m, k), F32),
        compiler_params=_params("parallel"),
        name="final_norm",
    )(x, gain.reshape(1, k))


def _rope_tables(n_pos, dim):
    inv = 1.0 / (ROPE_THETA ** (jnp.arange(0, dim, 2, dtype=F32) / dim))
    ang = jnp.arange(n_pos, dtype=F32)[:, None] * inv[None, :]
    return jnp.cos(ang), jnp.sin(ang)


def _rotate_half_columns(w):
    half = w.shape[1] // 2
    return jnp.concatenate([-w[:, half:], w[:, :half]], axis=1)


def _pad_cols(w, width):
    return jnp.pad(w, ((0, 0), (0, width - w.shape[1])))


def _mla_weights(w_down, w_uq, w_ukv):
    lat = MLA_Q_RANK + MLA_KV_RANK
    w_rope = w_down[:, lat:]
    w_down_ext = jnp.concatenate(
        [w_down[:, :lat], _pad_cols(w_rope, LANES), _pad_cols(_rotate_half_columns(w_rope), LANES)],
        axis=1).astype(BF16)
    wq = w_uq.reshape(MLA_Q_RANK, MLA_HEADS, MLA_NOPE_DIM + MLA_ROPE_DIM)
    heads = []
    for h in range(MLA_HEADS):
        w_r = wq[:, h, MLA_NOPE_DIM:]
        heads += [wq[:, h, :MLA_NOPE_DIM], _pad_cols(w_r, LANES),
                  _pad_cols(_rotate_half_columns(w_r), LANES)]
    wq_ext = jnp.concatenate(heads, axis=1).astype(BF16)
    wkv = w_ukv.reshape(MLA_KV_RANK, MLA_HEADS, MLA_NOPE_DIM + MLA_V_DIM)
    wkv = jnp.concatenate([wkv[:, :, :MLA_NOPE_DIM].reshape(MLA_KV_RANK, -1),
                           wkv[:, :, MLA_NOPE_DIM:].reshape(MLA_KV_RANK, -1)], axis=1).astype(BF16)
    return w_down_ext, wq_ext, wkv


def kernel(x, attn_norm, ffn_norm, final_norm, moba_w_qkv, moba_w_o, mla_w_down, mla_q_norm,
           mla_kv_norm, mla_w_uq, mla_w_ukv, mla_w_o, dil_w_qkv, dil_w_o, ffn_w_up, ffn_conv_w,
           ffn_conv_b, ffn_w_down):
    b, s, d = x.shape
    assert b == 1 and d == D_MODEL
    cos, sin = _rope_tables(s, HEAD_DIM)
    cos_full = jnp.concatenate([cos, cos], axis=1)
    sin_signed = jnp.concatenate([-sin, sin], axis=1)
    cos_r, sin_r = _rope_tables(s, MLA_ROPE_DIM)
    cos_r = _pad_cols(jnp.concatenate([cos_r, cos_r], axis=1), LANES)
    sin_r = _pad_cols(jnp.concatenate([sin_r, sin_r], axis=1), LANES)

    qkv_w = {0: moba_w_qkv.astype(BF16), 2: dil_w_qkv.astype(BF16)}
    out_w = {0: moba_w_o.astype(BF16), 1: mla_w_o.astype(BF16), 2: dil_w_o.astype(BF16)}
    down_w = ffn_w_down.astype(BF16)

    h = x.reshape(s, d)
    for i in range(DEPTH):
        j = i // N_MIXERS
        kind = i % N_MIXERS
        if kind == 1:
            w_down_ext, wq_ext, wkv = _mla_weights(mla_w_down[j], mla_w_uq[j], mla_w_ukv[j])
            cq, ckv, k_rope = _mla_down(h, attn_norm[i], w_down_ext, mla_q_norm[j], mla_kv_norm[j],
                                        cos_r, sin_r)
            q_full, kv = _mla_up(cq, ckv, wq_ext, wkv, cos_r, sin_r)
            mix = _mla_attention(q_full, kv, k_rope)
        else:
            qkv = _qkv_projection(h, attn_norm[i], qkv_w[kind], j, cos_full, sin_signed)
            mix = _moba_attention(qkv) if kind == 0 else _dilated_attention(qkv)
        h = _matmul_residual(mix, out_w[kind], j, h, tn=1024)
        act = _ffn_up(h, ffn_norm[i], ffn_w_up, ffn_conv_w, ffn_conv_b, i)
        h = _matmul_residual(act, down_w, i, h, tn=512)
    return _final_norm(h, final_norm).reshape(b, s, d)
```
